```python
import jax, jax.numpy as jnp
from jax import lax
import numpy as np

D_MODEL = 4096
BATCH = 8
SEQ = 4096
DEPTH = 1

ATT_HEAD_DIM = 128
ATT_Q_HEADS = 32
ATT_KV_HEADS = 8
ATT_GROUP = ATT_Q_HEADS // ATT_KV_HEADS
ATT_Q_DIM = ATT_Q_HEADS * ATT_HEAD_DIM
ATT_KV_DIM = ATT_KV_HEADS * ATT_HEAD_DIM
ATT_QKV_DIM = ATT_Q_DIM + 2 * ATT_KV_DIM
WINDOW = 128
BLOCK = 128
ROPE_THETA = 10000.0
NEG_INF = -1e30

RWKV_HEAD = 64
RWKV_DIM = D_MODEL
RWKV_HEADS = RWKV_DIM // RWKV_HEAD
D_DECAY = max(32, int(round(1.8 * RWKV_DIM ** 0.5 / 32)) * 32)
D_AAA = max(32, int(round(1.8 * RWKV_DIM ** 0.5 / 32)) * 32)
D_GATE = max(32, int(round(0.6 * RWKV_DIM ** 0.8 / 32)) * 32)
RWKV_SHIFT_DIM = 3 * RWKV_DIM + D_DECAY + D_AAA + D_GATE
RWKV_SPLITS = [RWKV_DIM, 2 * RWKV_DIM, 3 * RWKV_DIM, 3 * RWKV_DIM + D_DECAY, 3 * RWKV_DIM + D_DECAY + D_AAA]
GN_EPS = 64e-5

IN_DIM = ATT_QKV_DIM + RWKV_SHIFT_DIM + 2 * D_MODEL

FFN_DIM = ((8 * D_MODEL + 3 * 256 - 1) // (3 * 256)) * 256
RMS_EPS = 1e-6

kernel_name = 'hybrid_swa_sinks_rwkv7_gated_block'


def _rmsnorm(x, g):
    xf = x.astype(jnp.float32)
    y = xf * lax.rsqrt(jnp.mean(xf * xf, axis=-1, keepdims=True) + RMS_EPS)
    return (y * g.astype(jnp.float32)).astype(x.dtype)


def _rope_tables(seq, dtype):
    pos = jnp.arange(seq, dtype=jnp.float32)
    inv_freq = ROPE_THETA ** (-jnp.arange(0, ATT_HEAD_DIM, 2, dtype=jnp.float32) / ATT_HEAD_DIM)
    ang = pos[:, None] * inv_freq[None, :]
    return jnp.cos(ang).astype(dtype), jnp.sin(ang).astype(dtype)


def _rope(t, cos, sin):
    t1, t2 = jnp.split(t, 2, axis=-1)
    c = cos[None, :, None, :]
    s = sin[None, :, None, :]
    return jnp.concatenate([t1 * c - t2 * s, t2 * c + t1 * s], axis=-1)


def _swa_sinks(q, k, v, sinks):
    B, S = q.shape[0], q.shape[1]
    nb = S // BLOCK
    qb = q.reshape(B, nb, BLOCK, ATT_KV_HEADS, ATT_GROUP, ATT_HEAD_DIM)

    def band(t):
        tb = t.reshape(B, nb, BLOCK, ATT_KV_HEADS, ATT_HEAD_DIM)
        prev = jnp.pad(tb[:, :-1], ((0, 0), (1, 0), (0, 0), (0, 0), (0, 0)))
        return jnp.concatenate([prev, tb], axis=2)

    kb, vb = band(k), band(v)
    scores = jnp.einsum('bnqhgd,bnkhd->bnhgqk', qb, kb,
                        preferred_element_type=jnp.float32) * (ATT_HEAD_DIM ** -0.5)
    qi = jnp.arange(BLOCK)[:, None]
    kj = jnp.arange(2 * BLOCK)[None, :]
    rel = qi + BLOCK - kj
    key_pos = jnp.arange(nb)[:, None, None] * BLOCK + kj[None] - BLOCK
    mask = (rel >= 0) & (rel < WINDOW) & (key_pos >= 0)
    scores = jnp.where(mask[None, :, None, None], scores, NEG_INF)
    sink = sinks.astype(jnp.float32).reshape(ATT_KV_HEADS, ATT_GROUP)[None, None, :, :, None, None]
    m = jnp.maximum(jnp.max(scores, axis=-1, keepdims=True), sink)
    p = jnp.exp(scores - m)
    probs = p / (jnp.sum(p, axis=-1, keepdims=True) + jnp.exp(sink - m))
    o = jnp.einsum('bnhgqk,bnkhd->bnqhgd', probs.astype(v.dtype), vb)
    return o.reshape(B, S, ATT_Q_DIM)


def _wkv7_step(state, inp):
    r, w, k, v, a, b = inp
    sa = jnp.einsum('bhvk,bhk->bhv', state, a)
    state = state * w[:, :, None, :] + sa[..., None] * b[:, :, None, :] + v[..., None] * k[:, :, None, :]
    y = jnp.einsum('bhvk,bhk->bhv', state, r)
    return state, y


def _rwkv7(p_r, p_k, p_v, p_w, p_a, p_g, w0, w2, a0, a2, g2, k_k, k_a, r_k, ln_w, ln_b):
    B, S, C = p_r.shape
    H, N = RWKV_HEADS, RWKV_HEAD
    f32 = jnp.float32
    heads = lambda t: t.astype(f32).reshape(B, S, H, N)
    w = -jax.nn.softplus(-(w0 + jnp.tanh(p_w) @ w2)) - 0.5
    a = jax.nn.sigmoid(a0 + p_a @ a2)
    g = jax.nn.sigmoid(p_g) @ g2
    kk = heads(p_k * k_k)
    kk = kk / jnp.maximum(jnp.sqrt(jnp.sum(kk * kk, axis=-1, keepdims=True)), 1e-12)
    k = heads(p_k * (1.0 + (a - 1.0) * k_a))
    r = heads(p_r)
    v = heads(p_v)
    decay = jnp.exp(-jnp.exp(heads(w)))
    a_vec = -kk
    b_vec = kk * heads(a)
    tm = lambda t: jnp.moveaxis(t, 1, 0)
    state0 = jnp.zeros((B, H, N, N), f32)
    _, y = lax.scan(_wkv7_step, state0, (tm(r), tm(decay), tm(k), tm(v), tm(a_vec), tm(b_vec)))
    y = jnp.moveaxis(y, 0, 1)
    mu = jnp.mean(y, axis=-1, keepdims=True)
    var = jnp.mean(jnp.square(y - mu), axis=-1, keepdims=True)
    y = ((y - mu) * lax.rsqrt(var + GN_EPS)).reshape(B, S, C)
    y = y * ln_w.astype(f32) + ln_b.astype(f32)
    bonus = (jnp.sum(r * k * r_k.astype(f32), axis=-1, keepdims=True) * v).reshape(B, S, C)
    return ((y + bonus) * g.astype(f32)).astype(p_r.dtype)


def _layer(x, norm_mix_pre, norm_mix_post, norm_ffn_pre, norm_ffn_post, w_in, b_qkv, att_sinks,
           mu_shift, w0, w2, a0, a2, g2, k_k, k_a, r_k, ln_x_w, ln_x_b,
           w_att_branch, w_rwkv_branch, w_out, w_ffn_gate, w_ffn_up, w_ffn_down):
    B, S, _ = x.shape
    h = _rmsnorm(x, norm_mix_pre)
    proj = h @ w_in
    att_cols, rwkv_cols, gate_cols = jnp.split(
        proj, [ATT_QKV_DIM, ATT_QKV_DIM + RWKV_SHIFT_DIM], axis=-1)

    att_cols = att_cols + b_qkv
    q, k, v = jnp.split(att_cols, [ATT_Q_DIM, ATT_Q_DIM + ATT_KV_DIM], axis=-1)
    q = q.reshape(B, S, ATT_Q_HEADS, ATT_HEAD_DIM)
    k = k.reshape(B, S, ATT_KV_HEADS, ATT_HEAD_DIM)
    v = v.reshape(B, S, ATT_KV_HEADS, ATT_HEAD_DIM)
    cos, sin = _rope_tables(S, q.dtype)
    o_att = _swa_sinks(_rope(q, cos, sin), _rope(k, cos, sin), v, att_sinks)

    prev = jnp.pad(rwkv_cols[:, :-1], ((0, 0), (1, 0), (0, 0)))
    rwkv_cols = rwkv_cols + (prev - rwkv_cols) * mu_shift
    p_r, p_k, p_v, p_w, p_a, p_g = jnp.split(rwkv_cols, RWKV_SPLITS, axis=-1)
    o_rwkv = _rwkv7(p_r, p_k, p_v, p_w, p_a, p_g, w0, w2, a0, a2, g2, k_k, k_a, r_k, ln_x_w, ln_x_b)

    g_att, g_rwkv = jnp.split(gate_cols, 2, axis=-1)
    merged = (jax.nn.sigmoid(g_att) * (o_att @ w_att_branch)
              + jax.nn.sigmoid(g_rwkv) * (o_rwkv @ w_rwkv_branch))
    x = x + _rmsnorm(merged @ w_out, norm_mix_post)

    h = _rmsnorm(x, norm_ffn_pre)
    f = (jax.nn.silu(h @ w_ffn_gate) * (h @ w_ffn_up)) @ w_ffn_down
    return x + _rmsnorm(f, norm_ffn_post)


def _normal(k, shape, scale):
    return jax.random.normal(k, shape, jnp.float32) * scale


def _fwd_setup_inputs(seed: int = 0) -> dict:
    key = jax.random.key(seed)
    ks = jax.random.split(key, 26)
    L = DEPTH
    gain = lambda k: 1.0 + _normal(k, (L, D_MODEL), 0.02)
    return {
        'x': _normal(ks[0], (BATCH, SEQ, D_MODEL), 1.0),
        'norm_mix_pre': gain(ks[1]),
        'norm_mix_post': gain(ks[2]),
        'norm_ffn_pre': gain(ks[3]),
        'norm_ffn_post': gain(ks[4]),
        'w_in': _normal(ks[5], (L, D_MODEL, IN_DIM), D_MODEL ** -0.5),
        'b_qkv': _normal(ks[6], (L, ATT_QKV_DIM), 0.02),
        'att_sinks': _normal(ks[7], (L, ATT_Q_HEADS), 0.5),
        'mu_shift': jax.random.uniform(ks[8], (L, RWKV_SHIFT_DIM), jnp.float32, 0.0, 1.0),
        'w0': jax.random.uniform(ks[9], (L, RWKV_DIM), jnp.float32, -3.0, 0.0),
        'w2': _normal(ks[10], (L, D_DECAY, RWKV_DIM), 0.1 * D_DECAY ** -0.5),
        'a0': _normal(ks[11], (L, RWKV_DIM), 0.1),
        'a2': _normal(ks[12], (L, D_AAA, RWKV_DIM), 0.5 * D_AAA ** -0.5),
        'g2': _normal(ks[13], (L, D_GATE, RWKV_DIM), D_GATE ** -0.5),
        'k_k': 0.85 + _normal(ks[14], (L, RWKV_DIM), 0.02),
        'k_a': 1.0 + _normal(ks[15], (L, RWKV_DIM), 0.02),
        'r_k': _normal(ks[16], (L, RWKV_HEADS, RWKV_HEAD), 0.1),
        'ln_x_w': 1.0 + _normal(ks[17], (L, RWKV_DIM), 0.02),
        'ln_x_b': _normal(ks[18], (L, RWKV_DIM), 0.02),
        'w_att_branch': _normal(ks[19], (L, ATT_Q_DIM, D_MODEL), ATT_Q_DIM ** -0.5),
        'w_rwkv_branch': _normal(ks[20], (L, RWKV_DIM, D_MODEL), RWKV_DIM ** -0.5),
        'w_out': _normal(ks[21], (L, D_MODEL, D_MODEL), D_MODEL ** -0.5),
        'w_ffn_gate': _normal(ks[22], (L, D_MODEL, FFN_DIM), D_MODEL ** -0.5),
        'w_ffn_up': _normal(ks[23], (L, D_MODEL, FFN_DIM), D_MODEL ** -0.5),
        'w_ffn_down': _normal(ks[24], (L, FFN_DIM, D_MODEL), FFN_DIM ** -0.5),
    }


def _fwd_reference(x, norm_mix_pre, norm_mix_post, norm_ffn_pre, norm_ffn_post, w_in, b_qkv, att_sinks,
              mu_shift, w0, w2, a0, a2, g2, k_k, k_a, r_k, ln_x_w, ln_x_b,
              w_att_branch, w_rwkv_branch, w_out, w_ffn_gate, w_ffn_up, w_ffn_down):
    for l in range(DEPTH):
        x = _layer(x, norm_mix_pre[l], norm_mix_post[l], norm_ffn_pre[l], norm_ffn_post[l],
                   w_in[l], b_qkv[l], att_sinks[l], mu_shift[l], w0[l], w2[l], a0[l], a2[l],
                   g2[l], k_k[l], k_a[l], r_k[l], ln_x_w[l], ln_x_b[l],
                   w_att_branch[l], w_rwkv_branch[l], w_out[l],
                   w_ffn_gate[l], w_ffn_up[l], w_ffn_down[l])
    return x


import jax as _jax
import jax.numpy as _jnp

TWIN_FORMAT = 'train_step'
FWD_PARAMS = ['x', 'norm_mix_pre', 'norm_mix_post', 'norm_ffn_pre', 'norm_ffn_post', 'w_in', 'b_qkv', 'att_sinks', 'mu_shift', 'w0', 'w2', 'a0', 'a2', 'g2', 'k_k', 'k_a', 'r_k', 'ln_x_w', 'ln_x_b', 'w_att_branch', 'w_rwkv_branch', 'w_out', 'w_ffn_gate', 'w_ffn_up', 'w_ffn_down']
TWIN_WEIGHTS = ['norm_mix_pre', 'norm_mix_post', 'norm_ffn_pre', 'norm_ffn_post', 'w_in', 'b_qkv', 'att_sinks', 'mu_shift', 'w0', 'w2', 'a0', 'a2', 'g2', 'k_k', 'k_a', 'r_k', 'ln_x_w', 'ln_x_b', 'w_att_branch', 'w_rwkv_branch', 'w_out', 'w_ffn_gate', 'w_ffn_up', 'w_ffn_down']
TWIN_DIFF_INPUT = 'x'
TWIN_INPUTS = ['x', 'norm_mix_pre', 'norm_mix_post', 'norm_ffn_pre', 'norm_ffn_post', 'w_in', 'b_qkv', 'att_sinks', 'mu_shift', 'w0', 'w2', 'a0', 'a2', 'g2', 'k_k', 'k_a', 'r_k', 'ln_x_w', 'ln_x_b', 'w_att_branch', 'w_rwkv_branch', 'w_out', 'w_ffn_gate', 'w_ffn_up', 'w_ffn_down', 'loss_target', 'm_norm_mix_pre', 'm_norm_mix_post', 'm_norm_ffn_pre', 'm_norm_ffn_post', 'm_w_in', 'm_b_qkv', 'm_att_sinks', 'm_mu_shift', 'm_w0', 'm_w2', 'm_a0', 'm_a2', 'm_g2', 'm_k_k', 'm_k_a', 'm_r_k', 'm_ln_x_w', 'm_ln_x_b', 'm_w_att_branch', 'm_w_rwkv_branch', 'm_w_out', 'm_w_ffn_gate', 'm_w_ffn_up', 'm_w_ffn_down', 'v_norm_mix_pre', 'v_norm_mix_post', 'v_norm_ffn_pre', 'v_norm_ffn_post', 'v_w_in', 'v_b_qkv', 'v_att_sinks', 'v_mu_shift', 'v_w0', 'v_w2', 'v_a0', 'v_a2', 'v_g2', 'v_k_k', 'v_k_a', 'v_r_k', 'v_ln_x_w', 'v_ln_x_b', 'v_w_att_branch', 'v_w_rwkv_branch', 'v_w_out', 'v_w_ffn_gate', 'v_w_ffn_up', 'v_w_ffn_down']
TWIN_OUTPUTS = ['loss', 'grad_x', 'grad_norm_mix_pre', 'grad_norm_mix_post', 'grad_norm_ffn_pre', 'grad_norm_ffn_post', 'grad_w_in', 'grad_b_qkv', 'grad_att_sinks', 'grad_mu_shift', 'grad_w0', 'grad_w2', 'grad_a0', 'grad_a2', 'grad_g2', 'grad_k_k', 'grad_k_a', 'grad_r_k', 'grad_ln_x_w', 'grad_ln_x_b', 'grad_w_att_branch', 'grad_w_rwkv_branch', 'grad_w_out', 'grad_w_ffn_gate', 'grad_w_ffn_up', 'grad_w_ffn_down', 'delta_norm_mix_pre', 'delta_norm_mix_post', 'delta_norm_ffn_pre', 'delta_norm_ffn_post', 'delta_w_in', 'delta_b_qkv', 'delta_att_sinks', 'delta_mu_shift', 'delta_w0', 'delta_w2', 'delta_a0', 'delta_a2', 'delta_g2', 'delta_k_k', 'delta_k_a', 'delta_r_k', 'delta_ln_x_w', 'delta_ln_x_b', 'delta_w_att_branch', 'delta_w_rwkv_branch', 'delta_w_out', 'delta_w_ffn_gate', 'delta_w_ffn_up', 'delta_w_ffn_down', 'new_m_norm_mix_pre', 'new_m_norm_mix_post', 'new_m_norm_ffn_pre', 'new_m_norm_ffn_post', 'new_m_w_in', 'new_m_b_qkv', 'new_m_att_sinks', 'new_m_mu_shift', 'new_m_w0', 'new_m_w2', 'new_m_a0', 'new_m_a2', 'new_m_g2', 'new_m_k_k', 'new_m_k_a', 'new_m_r_k', 'new_m_ln_x_w', 'new_m_ln_x_b', 'new_m_w_att_branch', 'new_m_w_rwkv_branch', 'new_m_w_out', 'new_m_w_ffn_gate', 'new_m_w_ffn_up', 'new_m_w_ffn_down', 'new_v_norm_mix_pre', 'new_v_norm_mix_post', 'new_v_norm_ffn_pre', 'new_v_norm_ffn_post', 'new_v_w_in', 'new_v_b_qkv', 'new_v_att_sinks', 'new_v_mu_shift', 'new_v_w0', 'new_v_w2', 'new_v_a0', 'new_v_a2', 'new_v_g2', 'new_v_k_k', 'new_v_k_a', 'new_v_r_k', 'new_v_ln_x_w', 'new_v_ln_x_b', 'new_v_w_att_branch', 'new_v_w_rwkv_branch', 'new_v_w_out', 'new_v_w_ffn_gate', 'new_v_w_ffn_up', 'new_v_w_ffn_down']
TWIN_LEAF_KINDS = {'loss': 'loss', 'grad_x': 'grad_x', 'grad_norm_mix_pre': 'grad_w', 'grad_norm_mix_post': 'grad_w', 'grad_norm_ffn_pre': 'grad_w', 'grad_norm_ffn_post': 'grad_w', 'grad_w_in': 'grad_w', 'grad_b_qkv': 'grad_w', 'grad_att_sinks': 'grad_w', 'grad_mu_shift': 'grad_w', 'grad_w0': 'grad_w', 'grad_w2': 'grad_w', 'grad_a0': 'grad_w', 'grad_a2': 'grad_w', 'grad_g2': 'grad_w', 'grad_k_k': 'grad_w', 'grad_k_a': 'grad_w', 'grad_r_k': 'grad_w', 'grad_ln_x_w': 'grad_w', 'grad_ln_x_b': 'grad_w', 'grad_w_att_branch': 'grad_w', 'grad_w_rwkv_branch': 'grad_w', 'grad_w_out': 'grad_w', 'grad_w_ffn_gate': 'grad_w', 'grad_w_ffn_up': 'grad_w', 'grad_w_ffn_down': 'grad_w', 'delta_norm_mix_pre': 'delta_w', 'delta_norm_mix_post': 'delta_w', 'delta_norm_ffn_pre': 'delta_w', 'delta_norm_ffn_post': 'delta_w', 'delta_w_in': 'delta_w', 'delta_b_qkv': 'delta_w', 'delta_att_sinks': 'delta_w', 'delta_mu_shift': 'delta_w', 'delta_w0': 'delta_w', 'delta_w2': 'delta_w', 'delta_a0': 'delta_w', 'delta_a2': 'delta_w', 'delta_g2': 'delta_w', 'delta_k_k': 'delta_w', 'delta_k_a': 'delta_w', 'delta_r_k': 'delta_w', 'delta_ln_x_w': 'delta_w', 'delta_ln_x_b': 'delta_w', 'delta_w_att_branch': 'delta_w', 'delta_w_rwkv_branch': 'delta_w', 'delta_w_out': 'delta_w', 'delta_w_ffn_gate': 'delta_w', 'delta_w_ffn_up': 'delta_w', 'delta_w_ffn_down': 'delta_w', 'new_m_norm_mix_pre': 'new_m', 'new_m_norm_mix_post': 'new_m', 'new_m_norm_ffn_pre': 'new_m', 'new_m_norm_ffn_post': 'new_m', 'new_m_w_in': 'new_m', 'new_m_b_qkv': 'new_m', 'new_m_att_sinks': 'new_m', 'new_m_mu_shift': 'new_m', 'new_m_w0': 'new_m', 'new_m_w2': 'new_m', 'new_m_a0': 'new_m', 'new_m_a2': 'new_m', 'new_m_g2': 'new_m', 'new_m_k_k': 'new_m', 'new_m_k_a': 'new_m', 'new_m_r_k': 'new_m', 'new_m_ln_x_w': 'new_m', 'new_m_ln_x_b': 'new_m', 'new_m_w_att_branch': 'new_m', 'new_m_w_rwkv_branch': 'new_m', 'new_m_w_out': 'new_m', 'new_m_w_ffn_gate': 'new_m', 'new_m_w_ffn_up': 'new_m', 'new_m_w_ffn_down': 'new_m', 'new_v_norm_mix_pre': 'new_v', 'new_v_norm_mix_post': 'new_v', 'new_v_norm_ffn_pre': 'new_v', 'new_v_norm_ffn_post': 'new_v', 'new_v_w_in': 'new_v', 'new_v_b_qkv': 'new_v', 'new_v_att_sinks': 'new_v', 'new_v_mu_shift': 'new_v', 'new_v_w0': 'new_v', 'new_v_w2': 'new_v', 'new_v_a0': 'new_v', 'new_v_a2': 'new_v', 'new_v_g2': 'new_v', 'new_v_k_k': 'new_v', 'new_v_k_a': 'new_v', 'new_v_r_k': 'new_v', 'new_v_ln_x_w': 'new_v', 'new_v_ln_x_b': 'new_v', 'new_v_w_att_branch': 'new_v', 'new_v_w_rwkv_branch': 'new_v', 'new_v_w_out': 'new_v', 'new_v_w_ffn_gate': 'new_v', 'new_v_w_ffn_up': 'new_v', 'new_v_w_ffn_down': 'new_v'}


def _forward(args):
    return _fwd_reference(*[args[k] for k in FWD_PARAMS])


def _output_shape():
    out = _jax.eval_shape(lambda: _forward(_fwd_setup_inputs(0)))
    return out.shape, out.dtype

N_MICROBATCH = 1
ADAM_LR = 0.001
ADAM_B1 = 0.9
ADAM_B2 = 0.999
ADAM_EPS = 1e-08
ADAM_WD = 0.01
ADAM_STEP = 10
PER_EXAMPLE_BATCH_AXIS = {'x': 0, 'loss_target': 0}
SHARED_INPUTS = []
_WEIGHT_DTYPES = {'norm_mix_pre': _jnp.float32, 'norm_mix_post': _jnp.float32, 'norm_ffn_pre': _jnp.float32, 'norm_ffn_post': _jnp.float32, 'w_in': _jnp.float32, 'b_qkv': _jnp.float32, 'att_sinks': _jnp.float32, 'mu_shift': _jnp.float32, 'w0': _jnp.float32, 'w2': _jnp.float32, 'a0': _jnp.float32, 'a2': _jnp.float32, 'g2': _jnp.float32, 'k_k': _jnp.float32, 'k_a': _jnp.float32, 'r_k': _jnp.float32, 'ln_x_w': _jnp.float32, 'ln_x_b': _jnp.float32, 'w_att_branch': _jnp.float32, 'w_rwkv_branch': _jnp.float32, 'w_out': _jnp.float32, 'w_ffn_gate': _jnp.float32, 'w_ffn_up': _jnp.float32, 'w_ffn_down': _jnp.float32}
MOMENT_SCALE = {'norm_mix_pre': 2.085176e-01, 'norm_mix_post': 7.994451e+00, 'norm_ffn_pre': 1.133512e-01, 'norm_ffn_post': 8.002339e+00, 'w_in': 8.058290e-02, 'b_qkv': 2.411864e-01, 'att_sinks': 2.165270e-02, 'mu_shift': 1.904664e-01, 'w0': 5.614357e-02, 'w2': 1.126735e-02, 'a0': 4.854282e-02, 'a2': 4.266424e-02, 'g2': 1.066008e-01, 'k_k': 4.017092e-02, 'k_a': 1.153794e-01, 'r_k': 2.363971e-01, 'ln_x_w': 1.107765e-01, 'ln_x_b': 1.691995e-01, 'w_att_branch': 3.671471e-02, 'w_rwkv_branch': 1.122281e-01, 'w_out': 1.156782e-01, 'w_ffn_gate': 4.837338e-02, 'w_ffn_up': 5.326521e-02, 'w_ffn_down': 8.754863e-02}


def _to_microbatches(a, axis):
    t = _jnp.moveaxis(a, axis, 0)
    t = t.reshape((N_MICROBATCH, t.shape[0] // N_MICROBATCH) + t.shape[1:])
    return _jnp.moveaxis(t, 1, axis + 1)


def setup_inputs(seed: int = 0) -> dict:
    inp = _fwd_setup_inputs(seed)
    key = _jax.random.fold_in(_jax.random.key(seed), 7919)
    shape, _ = _output_shape()
    out = dict(inp)
    out["loss_target"] = _jax.random.normal(_jax.random.fold_in(key, 0), shape, _jnp.float32)
    for i, name in enumerate(TWIN_WEIGHTS):
        w = inp[name].astype(_jnp.float32)
        if MOMENT_SCALE is None:
            s = _jnp.sqrt(_jnp.mean(_jnp.square(w)) + 1e-30)
        else:
            s = MOMENT_SCALE[name]
        km, kv = _jax.random.split(_jax.random.fold_in(key, i + 1))
        out[name] = w
        out["m_" + name] = s * _jax.random.normal(km, w.shape, _jnp.float32)
        out["v_" + name] = (s * s) * _jax.random.uniform(kv, w.shape, _jnp.float32, 0.5, 1.5)
    if N_MICROBATCH > 1:
        for name, axis in PER_EXAMPLE_BATCH_AXIS.items():
            out[name] = _to_microbatches(out[name], axis)
    return {'x': out['x'], 'norm_mix_pre': out['norm_mix_pre'], 'norm_mix_post': out['norm_mix_post'], 'norm_ffn_pre': out['norm_ffn_pre'], 'norm_ffn_post': out['norm_ffn_post'], 'w_in': out['w_in'], 'b_qkv': out['b_qkv'], 'att_sinks': out['att_sinks'], 'mu_shift': out['mu_shift'], 'w0': out['w0'], 'w2': out['w2'], 'a0': out['a0'], 'a2': out['a2'], 'g2': out['g2'], 'k_k': out['k_k'], 'k_a': out['k_a'], 'r_k': out['r_k'], 'ln_x_w': out['ln_x_w'], 'ln_x_b': out['ln_x_b'], 'w_att_branch': out['w_att_branch'], 'w_rwkv_branch': out['w_rwkv_branch'], 'w_out': out['w_out'], 'w_ffn_gate': out['w_ffn_gate'], 'w_ffn_up': out['w_ffn_up'], 'w_ffn_down': out['w_ffn_down'], 'loss_target': out['loss_target'], 'm_norm_mix_pre': out['m_norm_mix_pre'], 'm_norm_mix_post': out['m_norm_mix_post'], 'm_norm_ffn_pre': out['m_norm_ffn_pre'], 'm_norm_ffn_post': out['m_norm_ffn_post'], 'm_w_in': out['m_w_in'], 'm_b_qkv': out['m_b_qkv'], 'm_att_sinks': out['m_att_sinks'], 'm_mu_shift': out['m_mu_shift'], 'm_w0': out['m_w0'], 'm_w2': out['m_w2'], 'm_a0': out['m_a0'], 'm_a2': out['m_a2'], 'm_g2': out['m_g2'], 'm_k_k': out['m_k_k'], 'm_k_a': out['m_k_a'], 'm_r_k': out['m_r_k'], 'm_ln_x_w': out['m_ln_x_w'], 'm_ln_x_b': out['m_ln_x_b'], 'm_w_att_branch': out['m_w_att_branch'], 'm_w_rwkv_branch': out['m_w_rwkv_branch'], 'm_w_out': out['m_w_out'], 'm_w_ffn_gate': out['m_w_ffn_gate'], 'm_w_ffn_up': out['m_w_ffn_up'], 'm_w_ffn_down': out['m_w_ffn_down'], 'v_norm_mix_pre': out['v_norm_mix_pre'], 'v_norm_mix_post': out['v_norm_mix_post'], 'v_norm_ffn_pre': out['v_norm_ffn_pre'], 'v_norm_ffn_post': out['v_norm_ffn_post'], 'v_w_in': out['v_w_in'], 'v_b_qkv': out['v_b_qkv'], 'v_att_sinks': out['v_att_sinks'], 'v_mu_shift': out['v_mu_shift'], 'v_w0': out['v_w0'], 'v_w2': out['v_w2'], 'v_a0': out['v_a0'], 'v_a2': out['v_a2'], 'v_g2': out['v_g2'], 'v_k_k': out['v_k_k'], 'v_k_a': out['v_k_a'], 'v_r_k': out['v_r_k'], 'v_ln_x_w': out['v_ln_x_w'], 'v_ln_x_b': out['v_ln_x_b'], 'v_w_att_branch': out['v_w_att_branch'], 'v_w_rwkv_branch': out['v_w_rwkv_branch'], 'v_w_out': out['v_w_out'], 'v_w_ffn_gate': out['v_w_ffn_gate'], 'v_w_ffn_up': out['v_w_ffn_up'], 'v_w_ffn_down': out['v_w_ffn_down']}


def _loss(weights, diff, rest, loss_target):
    with _jax.named_scope("forward"):
        args = {**rest, TWIN_DIFF_INPUT: diff, **{k: w.astype(_WEIGHT_DTYPES[k]) for k, w in weights.items()}}
        y = _forward(args)
    with _jax.named_scope("loss_head"):
        err = _jnp.square(y.astype(_jnp.float32) - loss_target)
        return 0.5 * _jnp.sum(_jnp.mean(err, axis=-1)) if err.ndim else 0.5 * err


def _adamw(w, g, m, v):
    m = ADAM_B1 * m + (1.0 - ADAM_B1) * g
    v = ADAM_B2 * v + (1.0 - ADAM_B2) * _jnp.square(g)
    m_hat = m / (1.0 - ADAM_B1 ** ADAM_STEP)
    v_hat = v / (1.0 - ADAM_B2 ** ADAM_STEP)
    delta = -ADAM_LR * (m_hat / (_jnp.sqrt(v_hat) + ADAM_EPS) + ADAM_WD * w)
    return delta, m, v


def reference(x, norm_mix_pre, norm_mix_post, norm_ffn_pre, norm_ffn_post, w_in, b_qkv, att_sinks, mu_shift, w0, w2, a0, a2, g2, k_k, k_a, r_k, ln_x_w, ln_x_b, w_att_branch, w_rwkv_branch, w_out, w_ffn_gate, w_ffn_up, w_ffn_down, loss_target, m_norm_mix_pre, m_norm_mix_post, m_norm_ffn_pre, m_norm_ffn_post, m_w_in, m_b_qkv, m_att_sinks, m_mu_shift, m_w0, m_w2, m_a0, m_a2, m_g2, m_k_k, m_k_a, m_r_k, m_ln_x_w, m_ln_x_b, m_w_att_branch, m_w_rwkv_branch, m_w_out, m_w_ffn_gate, m_w_ffn_up, m_w_ffn_down, v_norm_mix_pre, v_norm_mix_post, v_norm_ffn_pre, v_norm_ffn_post, v_w_in, v_b_qkv, v_att_sinks, v_mu_shift, v_w0, v_w2, v_a0, v_a2, v_g2, v_k_k, v_k_a, v_r_k, v_ln_x_w, v_ln_x_b, v_w_att_branch, v_w_rwkv_branch, v_w_out, v_w_ffn_gate, v_w_ffn_up, v_w_ffn_down):
    given = dict(x=x, norm_mix_pre=norm_mix_pre, norm_mix_post=norm_mix_post, norm_ffn_pre=norm_ffn_pre, norm_ffn_post=norm_ffn_post, w_in=w_in, b_qkv=b_qkv, att_sinks=att_sinks, mu_shift=mu_shift, w0=w0, w2=w2, a0=a0, a2=a2, g2=g2, k_k=k_k, k_a=k_a, r_k=r_k, ln_x_w=ln_x_w, ln_x_b=ln_x_b, w_att_branch=w_att_branch, w_rwkv_branch=w_rwkv_branch, w_out=w_out, w_ffn_gate=w_ffn_gate, w_ffn_up=w_ffn_up, w_ffn_down=w_ffn_down, loss_target=loss_target, m_norm_mix_pre=m_norm_mix_pre, m_norm_mix_post=m_norm_mix_post, m_norm_ffn_pre=m_norm_ffn_pre, m_norm_ffn_post=m_norm_ffn_post, m_w_in=m_w_in, m_b_qkv=m_b_qkv, m_att_sinks=m_att_sinks, m_mu_shift=m_mu_shift, m_w0=m_w0, m_w2=m_w2, m_a0=m_a0, m_a2=m_a2, m_g2=m_g2, m_k_k=m_k_k, m_k_a=m_k_a, m_r_k=m_r_k, m_ln_x_w=m_ln_x_w, m_ln_x_b=m_ln_x_b, m_w_att_branch=m_w_att_branch, m_w_rwkv_branch=m_w_rwkv_branch, m_w_out=m_w_out, m_w_ffn_gate=m_w_ffn_gate, m_w_ffn_up=m_w_ffn_up, m_w_ffn_down=m_w_ffn_down, v_norm_mix_pre=v_norm_mix_pre, v_norm_mix_post=v_norm_mix_post, v_norm_ffn_pre=v_norm_ffn_pre, v_norm_ffn_post=v_norm_ffn_post, v_w_in=v_w_in, v_b_qkv=v_b_qkv, v_att_sinks=v_att_sinks, v_mu_shift=v_mu_shift, v_w0=v_w0, v_w2=v_w2, v_a0=v_a0, v_a2=v_a2, v_g2=v_g2, v_k_k=v_k_k, v_k_a=v_k_a, v_r_k=v_r_k, v_ln_x_w=v_ln_x_w, v_ln_x_b=v_ln_x_b, v_w_att_branch=v_w_att_branch, v_w_rwkv_branch=v_w_rwkv_branch, v_w_out=v_w_out, v_w_ffn_gate=v_w_ffn_gate, v_w_ffn_up=v_w_ffn_up, v_w_ffn_down=v_w_ffn_down)
    weights = {n: given[n] for n in TWIN_WEIGHTS}
    shared = {n: given[n] for n in SHARED_INPUTS}
    per_example = {n: given[n] for n in ['x']}
    grad_fn = _jax.value_and_grad(_loss, argnums=(0, 1))

    def one_microbatch(ex, loss_target):
        ex = dict(ex)
        diff = ex.pop(TWIN_DIFF_INPUT)
        return grad_fn(weights, diff, {**shared, **ex}, loss_target)

    if N_MICROBATCH == 1:
        loss, (grad_w, grad_x) = one_microbatch(per_example, given["loss_target"])
    else:
        def body(carry, xs):
            loss_sum, grad_sum = carry
            l_k, (gw_k, gx_k) = one_microbatch(xs[0], xs[1])
            with _jax.named_scope("update"):
                return (loss_sum + l_k, _jax.tree.map(_jnp.add, grad_sum, gw_k)), gx_k

        init = (_jnp.zeros((), _jnp.float32), _jax.tree.map(_jnp.zeros_like, weights))
        (loss, grad_w), grad_x = _jax.lax.scan(body, init, (per_example, given["loss_target"]))
    with _jax.named_scope("update"):
        delta_w, new_m, new_v = {}, {}, {}
        for n in TWIN_WEIGHTS:
            delta_w[n], new_m[n], new_v[n] = _adamw(weights[n], grad_w[n], given["m_" + n], given["v_" + n])
    return (loss, grad_x, *[grad_w[n] for n in TWIN_WEIGHTS], *[delta_w[n] for n in TWIN_WEIGHTS],
            *[new_m[n] for n in TWIN_WEIGHTS], *[new_v[n] for n in TWIN_WEIGHTS])
```

```python
import functools
import math

import jax
import jax.numpy as jnp
from jax import lax
from jax.experimental import pallas as pl
from jax.experimental.pallas import tpu as pltpu

f32 = jnp.float32
bf16 = jnp.bfloat16

N_DEV = 8
MESH_AXES = ("x", "y", "c")
LANES = 128
ATT_HEAD = 128
ATT_GROUP = 4
ATT_BLOCK = 128
ROPE_THETA = 10000.0
NEG_INF = -1e30
RWKV_HEAD = 64
RWKV_GROUP = 4
RWKV_CHUNK = 64
RWKV_LANES = 4 * RWKV_GROUP * RWKV_HEAD
RMS_EPS = 1e-6
GN_EPS = 64e-5
ADAM_LR, ADAM_B1, ADAM_B2, ADAM_EPS, ADAM_WD, ADAM_STEP = 0.001, 0.9, 0.999, 1e-08, 0.01, 10
VMEM_LIMIT = 56 * 1024 * 1024
MM_FULL_K = 4096
MM_VMEM_BUDGET = 40 * 1024 * 1024
SCAN_PREC = lax.Precision.HIGH
BULK_PREC = lax.Precision.HIGH
APPLY_PREC = lax.Precision.HIGH
CHAIN_PREC = lax.Precision.DEFAULT
MESH_ID = pl.DeviceIdType.MESH


def _cp(sem):
    return pltpu.CompilerParams(dimension_semantics=sem, vmem_limit_bytes=VMEM_LIMIT)


def _pick(n, cands):
    for c in cands:
        if n % c == 0:
            return c
    return n


def _round_up(n, m):
    return (n + m - 1) // m * m


def _mm(a, b, *, name, ta=False, tb=False, o_blk=False, out_dtype=f32, bias=None, add=None, a_col0=0, a_cols=None):
    a3 = a if a.ndim == 3 else a[None]
    b3 = b if b.ndim == 3 else b[None]
    ja, jb = a3.shape[0], b3.shape[0]
    nj = max(ja, jb)
    if ta:
        kdim, m = a3.shape[1:]
    else:
        m, kdim = a3.shape[1:]
    if a_cols is not None:
        kdim = a_cols
    if tb:
        n, kb = b3.shape[1:]
    else:
        kb, n = b3.shape[1:]
    assert kdim == kb, (name, a3.shape, b3.shape)
    reduce_j = nj > 1 and not o_blk
    tm = _pick(m, (1024, 512, 256, 128) if reduce_j else (512, 256, 128))
    tn = _pick(n, (1024, 512, 256, 128))
    tk = kdim if kdim <= MM_FULL_K else _pick(kdim, (2048, 1024, 512, 256, 128))
    osz = jnp.dtype(out_dtype).itemsize + (4 if add is not None else 0)
    vmem = lambda tn_, tk_: 2 * (a3.dtype.itemsize * tm * tk_ + b3.dtype.itemsize * tk_ * tn_ + osz * tm * tn_) + 8 * tm * tn_
    while vmem(tn, tk) > MM_VMEM_BUDGET and tn % 256 == 0:
        tn //= 2
    while vmem(tn, tk) > MM_VMEM_BUDGET and tk % 256 == 0:
        tk //= 2
    assert a_col0 % tk == 0
    kc0 = a_col0 // tk
    nm, nn, nk = m // tm, n // tn, kdim // tk
    if reduce_j:
        grid = (nm, nn, nj, nk)
        unpack = lambda i0, i1, i2, i3: (i2, i0, i1, i3)
        sem = ("parallel", "parallel", "arbitrary", "arbitrary")
    else:
        grid = (nj, nm, nn, nk)
        unpack = lambda i0, i1, i2, i3: (i0, i1, i2, i3)
        sem = ("parallel", "parallel", "parallel", "arbitrary")

    def a_map(*g):
        j, mi, ni, ki = unpack(*g)
        jj = j if ja > 1 else 0
        return (jj, ki, mi) if ta else (jj, mi, ki + kc0)

    def b_map(*g):
        j, mi, ni, ki = unpack(*g)
        jj = j if jb > 1 else 0
        return (jj, ni, ki) if tb else (jj, ki, ni)

    def o_map(*g):
        j, mi, ni, ki = unpack(*g)
        return (j if o_blk else 0, mi, ni)

    in_specs = [pl.BlockSpec((1, tk, tm) if ta else (1, tm, tk), a_map),
                pl.BlockSpec((1, tn, tk) if tb else (1, tk, tn), b_map)]
    args = [a3, b3]
    if bias is not None:
        in_specs.append(pl.BlockSpec((1, tn), lambda *g: (0, unpack(*g)[2])))
        args.append(bias)
    if add is not None:
        add3 = add if add.ndim == 3 else add[None]
        in_specs.append(pl.BlockSpec((1, tm, tn), o_map))
        args.append(add3)
    has_bias, has_add = bias is not None, add is not None
    single = nk == 1 and not reduce_j
    dims = (((0 if ta else 1,), (1 if tb else 0,)), ((), ()))

    def body(*refs):
        a_ref, b_ref = refs[0], refs[1]
        o_ref, acc_ref = (refs[-1], None) if single else (refs[-2], refs[-1])
        j, mi, ni, ki = unpack(pl.program_id(0), pl.program_id(1), pl.program_id(2), pl.program_id(3))
        first = ki == 0
        last = ki == nk - 1
        if reduce_j:
            first = jnp.logical_and(first, j == 0)
            last = jnp.logical_and(last, j == nj - 1)

        prod = lax.dot_general(a_ref[0].astype(bf16), b_ref[0].astype(bf16), dims, preferred_element_type=f32)

        def finish(r):
            idx = 2
            if has_bias:
                r = r + refs[idx][...]
                idx += 1
            if has_add:
                r = r + refs[idx][0].astype(f32)
            o_ref[0] = r.astype(out_dtype)

        if single:
            finish(prod)
        else:
            @pl.when(first)
            def _():
                acc_ref[...] = prod

            @pl.when(jnp.logical_not(first))
            def _():
                acc_ref[...] += prod

            @pl.when(last)
            def _():
                finish(acc_ref[...])

    jo = nj if o_blk else 1
    out = pl.pallas_call(
        body, name=name, grid=grid, in_specs=in_specs, out_specs=pl.BlockSpec((1, tm, tn), o_map),
        out_shape=jax.ShapeDtypeStruct((jo, m, n), out_dtype), scratch_shapes=[] if single else [pltpu.VMEM((tm, tn), f32)],
        compiler_params=_cp(sem),
    )(*args)
    return out if o_blk else out[0]


def _rowwise(fn, rows, consts, out_rows, out_accs, *, tm, name):
    nrow = rows[0].shape[0]
    assert all(r.shape[0] == nrow for r in rows) and nrow % tm == 0
    nr, nc, no, na = len(rows), len(consts), len(out_rows), len(out_accs)

    def body(*refs):
        rin, cin = refs[:nr], refs[nr:nr + nc]
        orow, oacc = refs[nr + nc:nr + nc + no], refs[nr + nc + no:]
        ro, ao = fn(*[r[...] for r in rin], *[c[...] for c in cin])
        for r, v in zip(orow, ro):
            r[...] = v.astype(r.dtype)
        if na:
            @pl.when(pl.program_id(0) == 0)
            def _():
                for r in oacc:
                    r[...] = jnp.zeros_like(r)
            for r, v in zip(oacc, ao):
                r[...] += v

    in_specs = [pl.BlockSpec((tm, r.shape[1]), lambda i: (i, 0)) for r in rows]
    in_specs += [pl.BlockSpec(c.shape, lambda i, nd=c.ndim: (0,) * nd) for c in consts]
    out_specs = [pl.BlockSpec((tm, w), lambda i: (i, 0)) for w, _ in out_rows]
    out_specs += [pl.BlockSpec(s, lambda i, nd=len(s): (0,) * nd) for s in out_accs]
    out_shape = [jax.ShapeDtypeStruct((nrow, w), dt) for w, dt in out_rows]
    out_shape += [jax.ShapeDtypeStruct(s, f32) for s in out_accs]
    res = pl.pallas_call(body, name=name, grid=(nrow // tm,), in_specs=in_specs, out_specs=out_specs, out_shape=out_shape,
                         compiler_params=_cp(("arbitrary",)))(*rows, *consts)
    return res[:no], res[no:]


def _rms(x, g):
    return x * lax.rsqrt(jnp.mean(x * x, axis=-1, keepdims=True) + RMS_EPS) * g


def _sigmoid(x):
    return 1.0 / (1.0 + jnp.exp(-x))


def _colsum(v):
    return jnp.sum(v, axis=0, keepdims=True)


def _prev_rows(xv, edge_row, has_edge):
    rolled = pltpu.roll(xv, 1, 0)
    row = lax.broadcasted_iota(jnp.int32, xv.shape, 0)
    edge = jnp.where(has_edge, edge_row, jnp.zeros_like(edge_row))
    return jnp.where(row == 0, edge, rolled)


def _next_rows(xv, edge_row, has_edge):
    tm = xv.shape[0]
    rolled = pltpu.roll(xv, tm - 1, 0)
    row = lax.broadcasted_iota(jnp.int32, xv.shape, 0)
    edge = jnp.where(has_edge, edge_row, jnp.zeros_like(edge_row))
    return jnp.where(row == tm - 1, edge, rolled)


def _shift_fwd(x, mu, *, name, tm=256):
    s, c = x.shape
    tc = _pick(c, (2048, 1024, 512, 256, 128))
    g8 = tm // 8

    def body(x_ref, xp_ref, mu_ref, o_ref):
        i = pl.program_id(1)
        xv = x_ref[...]
        prev = _prev_rows(xv, xp_ref[7:8, :], i > 0)
        o_ref[...] = xv + (prev - xv) * mu_ref[...]

    return pl.pallas_call(
        body, name=name, grid=(c // tc, s // tm),
        in_specs=[pl.BlockSpec((tm, tc), lambda j, i: (i, j)),
                  pl.BlockSpec((8, tc), lambda j, i: (jnp.maximum(i * g8 - 1, 0), j)),
                  pl.BlockSpec((1, tc), lambda j, i: (0, j))],
        out_specs=pl.BlockSpec((tm, tc), lambda j, i: (i, j)),
        out_shape=jax.ShapeDtypeStruct((s, c), f32), compiler_params=_cp(("parallel", "arbitrary")),
    )(x, x, mu)


def _shift_bwd(x, col0, dxs, mu, *, name, out_dtype=bf16, tm=256):
    s, c = dxs.shape
    tc = _pick(c, (2048, 1024, 512, 256, 128))
    assert col0 % tc == 0
    jc0 = col0 // tc
    g8 = tm // 8
    ni = s // tm
    last8 = s // 8 - 1

    def body(x_ref, xp_ref, d_ref, dn_ref, mu_ref, dx_ref, dmu_ref):
        i = pl.program_id(1)
        xv = x_ref[...]
        prev = _prev_rows(xv, xp_ref[7:8, :], i > 0)
        dv = d_ref[...].astype(f32)
        dnext = _next_rows(dv, dn_ref[0:1, :].astype(f32), i < ni - 1)
        muv = mu_ref[...]
        dx_ref[...] = (dv * (1.0 - muv) + dnext * muv).astype(dx_ref.dtype)

        @pl.when(i == 0)
        def _():
            dmu_ref[...] = jnp.zeros_like(dmu_ref)

        dmu_ref[...] += _colsum(dv * (prev - xv))

    return pl.pallas_call(
        body, name=name, grid=(c // tc, ni),
        in_specs=[pl.BlockSpec((tm, tc), lambda j, i: (i, j + jc0)),
                  pl.BlockSpec((8, tc), lambda j, i: (jnp.maximum(i * g8 - 1, 0), j + jc0)),
                  pl.BlockSpec((tm, tc), lambda j, i: (i, j)),
                  pl.BlockSpec((8, tc), lambda j, i: (jnp.minimum((i + 1) * g8, last8), j)),
                  pl.BlockSpec((1, tc), lambda j, i: (0, j))],
        out_specs=[pl.BlockSpec((tm, tc), lambda j, i: (i, j)), pl.BlockSpec((1, tc), lambda j, i: (0, j))],
        out_shape=[jax.ShapeDtypeStruct((s, c), out_dtype), jax.ShapeDtypeStruct((1, c), f32)],
        compiler_params=_cp(("parallel", "arbitrary")),
    )(x, x, dxs, dxs, mu)


def _rope(t, cos, sin):
    h = ATT_HEAD // 2
    t1, t2 = t[:, :h], t[:, h:]
    return jnp.concatenate([t1 * cos - t2 * sin, t2 * cos + t1 * sin], axis=-1)


def _att_block(q, kc, kp, vc, vp, sinks, cos_c, sin_c, cos_p, sin_p, first_block):
    blk = ATT_BLOCK
    kk = jnp.concatenate([_rope(kp, cos_p, sin_p), _rope(kc, cos_c, sin_c)], axis=0).astype(bf16)
    vv = jnp.concatenate([vp, vc], axis=0).astype(bf16)
    qi = lax.broadcasted_iota(jnp.int32, (blk, 2 * blk), 0)
    kj = lax.broadcasted_iota(jnp.int32, (blk, 2 * blk), 1)
    rel = qi + blk - kj
    mask = (rel >= 0) & (rel < blk) & jnp.logical_or(kj >= blk, jnp.logical_not(first_block))
    outs = []
    for g in range(ATT_GROUP):
        qg = _rope(q[:, g * ATT_HEAD:(g + 1) * ATT_HEAD], cos_c, sin_c).astype(bf16)
        sc = lax.dot_general(qg, kk, (((1,), (1,)), ((), ())), preferred_element_type=f32) * (ATT_HEAD ** -0.5)
        sc = jnp.where(mask, sc, NEG_INF)
        sink = sinks[g]
        m = jnp.maximum(jnp.max(sc, axis=-1, keepdims=True), sink)
        p = jnp.exp(sc - m)
        probs = p / (jnp.sum(p, axis=-1, keepdims=True) + jnp.exp(sink - m))
        outs.append(jnp.dot(probs.astype(bf16), vv, preferred_element_type=f32))
    return jnp.concatenate(outs, axis=-1)


def _att_specs(qd, kvd):
    gw = ATT_GROUP * ATT_HEAD
    kb0, vb0 = qd // ATT_HEAD, (qd + kvd) // ATT_HEAD
    prev = lambda n: jnp.maximum(n - 1, 0)
    half = ATT_HEAD // 2
    return [
        pl.BlockSpec((ATT_BLOCK, gw), lambda h, n: (n, h)),
        pl.BlockSpec((ATT_BLOCK, ATT_HEAD), lambda h, n: (n, kb0 + h)),
        pl.BlockSpec((ATT_BLOCK, ATT_HEAD), lambda h, n: (prev(n), kb0 + h)),
        pl.BlockSpec((ATT_BLOCK, ATT_HEAD), lambda h, n: (n, vb0 + h)),
        pl.BlockSpec((ATT_BLOCK, ATT_HEAD), lambda h, n: (prev(n), vb0 + h)),
        pl.BlockSpec((1, ATT_GROUP, 1, 1), lambda h, n: (h, 0, 0, 0)),
        pl.BlockSpec((ATT_BLOCK, half), lambda h, n: (n, 0)),
        pl.BlockSpec((ATT_BLOCK, half), lambda h, n: (n, 0)),
        pl.BlockSpec((ATT_BLOCK, half), lambda h, n: (prev(n), 0)),
        pl.BlockSpec((ATT_BLOCK, half), lambda h, n: (prev(n), 0)),
    ]


def _att_fwd(qkv, sinks4, cos, sin, qd, kvd):
    s = qkv.shape[0]
    hkv = kvd // ATT_HEAD
    gw = ATT_GROUP * ATT_HEAD

    def body(q_ref, kc_ref, kp_ref, vc_ref, vp_ref, sk_ref, cc_ref, sc_ref, cp_ref, sp_ref, o_ref):
        sinks = [sk_ref[0, g] for g in range(ATT_GROUP)]
        o = _att_block(q_ref[...], kc_ref[...], kp_ref[...], vc_ref[...], vp_ref[...], sinks,
                       cc_ref[...], sc_ref[...], cp_ref[...], sp_ref[...], pl.program_id(1) == 0)
        o_ref[...] = o.astype(o_ref.dtype)

    return pl.pallas_call(
        body, name="att_fwd", grid=(hkv, s // ATT_BLOCK), in_specs=_att_specs(qd, kvd),
        out_specs=pl.BlockSpec((ATT_BLOCK, gw), lambda h, n: (n, h)),
        out_shape=jax.ShapeDtypeStruct((s, qd), bf16), compiler_params=_cp(("parallel", "arbitrary")),
    )(qkv, qkv, qkv, qkv, qkv, sinks4, cos, sin, cos, sin)


def _att_bwd(qkv, sinks4, cos, sin, do, qd, kvd):
    s = qkv.shape[0]
    hkv = kvd // ATT_HEAD
    gw = ATT_GROUP * ATT_HEAD

    def body(q_ref, kc_ref, kp_ref, vc_ref, vp_ref, sk_ref, cc_ref, sc_ref, cp_ref, sp_ref, do_ref,
             dq_ref, dkc_ref, dkp_ref, dvc_ref, dvp_ref, dsk_ref):
        n = pl.program_id(1)
        cc, sc, cp, sp = cc_ref[...], sc_ref[...], cp_ref[...], sp_ref[...]

        def f(q, kc, kp, vc, vp, *sinks):
            return _att_block(q, kc, kp, vc, vp, sinks, cc, sc, cp, sp, n == 0)

        sinks = [sk_ref[0, g] for g in range(ATT_GROUP)]
        _, vjp = jax.vjp(f, q_ref[...], kc_ref[...], kp_ref[...], vc_ref[...], vp_ref[...], *sinks)
        ct = vjp(do_ref[...].astype(f32))
        dq_ref[...] = ct[0].astype(dq_ref.dtype)
        dkc_ref[...] = ct[1].astype(dkc_ref.dtype)
        dkp_ref[...] = ct[2].astype(dkp_ref.dtype)
        dvc_ref[...] = ct[3].astype(dvc_ref.dtype)
        dvp_ref[...] = ct[4].astype(dvp_ref.dtype)

        @pl.when(n == 0)
        def _():
            dsk_ref[...] = jnp.zeros_like(dsk_ref)

        for g in range(ATT_GROUP):
            dsk_ref[0, g] += ct[5 + g]

    kv_spec = pl.BlockSpec((ATT_BLOCK, ATT_HEAD), lambda h, n: (n, h))
    return pl.pallas_call(
        body, name="att_bwd", grid=(hkv, s // ATT_BLOCK),
        in_specs=_att_specs(qd, kvd) + [pl.BlockSpec((ATT_BLOCK, gw), lambda h, n: (n, h))],
        out_specs=[pl.BlockSpec((ATT_BLOCK, gw), lambda h, n: (n, h)), kv_spec, kv_spec, kv_spec, kv_spec,
                   pl.BlockSpec((1, ATT_GROUP, 1, 1), lambda h, n: (h, 0, 0, 0))],
        out_shape=[jax.ShapeDtypeStruct((s, qd), f32)] + [jax.ShapeDtypeStruct((s, kvd), f32)] * 4
        + [jax.ShapeDtypeStruct((hkv, ATT_GROUP, 1, 1), f32)],
        compiler_params=_cp(("parallel", "arbitrary")),
    )(qkv, qkv, qkv, qkv, qkv, sinks4, cos, sin, cos, sin, do)


def _att_dqkv(dq, dkc, dkp, dvc, dvp):
    s, qd = dq.shape
    kvd = dkc.shape[1]
    nb = s // ATT_BLOCK

    def body(dq_ref, dkc_ref, dkp_ref, dvc_ref, dvp_ref, o_ref, b_ref):
        n = pl.program_id(0)
        keep = n < nb - 1
        dk = dkc_ref[...] + jnp.where(keep, dkp_ref[...], 0.0)
        dv = dvc_ref[...] + jnp.where(keep, dvp_ref[...], 0.0)
        d = jnp.concatenate([dq_ref[...], dk, dv], axis=-1)
        o_ref[...] = d.astype(o_ref.dtype)

        @pl.when(n == 0)
        def _():
            b_ref[...] = jnp.zeros_like(b_ref)

        b_ref[...] += _colsum(d)

    cur = lambda w: pl.BlockSpec((ATT_BLOCK, w), lambda n: (n, 0))
    nxt = lambda w: pl.BlockSpec((ATT_BLOCK, w), lambda n: (jnp.minimum(n + 1, nb - 1), 0))
    w = qd + 2 * kvd
    return pl.pallas_call(
        body, name="att_dqkv", grid=(nb,), in_specs=[cur(qd), cur(kvd), nxt(kvd), cur(kvd), nxt(kvd)],
        out_specs=[cur(w), pl.BlockSpec((1, w), lambda n: (0, 0))],
        out_shape=[jax.ShapeDtypeStruct((s, w), bf16), jax.ShapeDtypeStruct((1, w), f32)],
        compiler_params=_cp(("arbitrary",)),
    )(dq, dkc, dkp, dvc, dvp)


def _sdot(a, b, dims):
    return lax.dot_general(a, b, (dims, ((), ())), precision=SCAN_PREC, preferred_element_type=f32)


def _cdot(a, b, dims):
    return lax.dot_general(a, b, (dims, ((), ())), precision=CHAIN_PREC, preferred_element_type=f32)


def _adot(a, b, dims):
    return lax.dot_general(a, b, (dims, ((), ())), precision=APPLY_PREC, preferred_element_type=f32)


def _bdot(a, b, dims):
    return lax.dot_general(a, b, (dims, ((), ())), precision=BULK_PREC, preferred_element_type=f32)


_NN = ((1,), (0,))
_NT = ((1,), (1,))
_TN = ((0,), (0,))


def _head_mask(wd):
    li = lax.broadcasted_iota(jnp.int32, (wd, wd), 0) // RWKV_HEAD
    lj = lax.broadcasted_iota(jnp.int32, (wd, wd), 1) // RWKV_HEAD
    return (li == lj).astype(f32)


def _stack_impl(t):
    wd = t.shape[1]
    return jnp.concatenate([t] * (wd // RWKV_HEAD), axis=0) * _head_mask(wd)


def _unstack_impl(m):
    wd = m.shape[1]
    m = m * _head_mask(wd)
    return sum(m[h * RWKV_HEAD:(h + 1) * RWKV_HEAD] for h in range(wd // RWKV_HEAD))


@jax.custom_vjp
def _stack(t):
    return _stack_impl(t)


@jax.custom_vjp
def _unstack(m):
    return _unstack_impl(m)


_stack.defvjp(lambda t: (_stack_impl(t), None), lambda _, dm: (_unstack_impl(dm),))
_unstack.defvjp(lambda m: (_unstack_impl(m), None), lambda _, dt: (_stack_impl(dt),))


def _unit_lower_inverse(t):
    L, wd = t.shape
    qt = lax.broadcasted_iota(jnp.int32, (L, wd), 0)
    qi = lax.broadcasted_iota(jnp.int32, (L, wd), 1) % L
    x = (qt == qi).astype(f32) + t
    p = _cdot(t, _stack(t), _NN)
    for _ in range(int(math.log2(L)) - 2):
        both = _cdot(p, jnp.concatenate([_stack(x), _stack(p)], axis=1), _NN)
        x, p = x + both[:, :wd], both[:, wd:]
    return x + _cdot(p, _stack(x), _NN)


@jax.custom_vjp
def _solve_unit_lower(t, rhs):
    return _cdot(_unit_lower_inverse(t), _stack(rhs), _NN)


def _solve_fwd(t, rhs):
    minv = _unit_lower_inverse(t)
    u = _cdot(minv, _stack(rhs), _NN)
    return u, (minv, u)


def _solve_bwd(res, du):
    minv, u = res
    drhs = _unstack(_cdot(minv, du, _TN))
    return _cdot(drhs, _stack(u), _NT), drhs


_solve_unit_lower.defvjp(_solve_fwd, _solve_bwd)


def _rwkv_chunk(s0, xr, xk, xv, wp, ap, g, w0, a0, k_k, k_a, ln_w, ln_b, r_k):
    L, wd = xr.shape
    n = RWKV_HEAD
    assert L == n
    hsum = lambda t: _sdot(t, _head_mask(wd), _NN)
    stack = _stack

    wl = -jax.nn.softplus(-(w0 + wp)) - 0.5
    a = _sigmoid(a0 + ap)
    kkv = xk * k_k
    k = xk * (1.0 + (a - 1.0) * k_a)
    early = hsum(jnp.concatenate([kkv * kkv, xr * k * r_k], axis=0))
    kk = kkv / jnp.maximum(jnp.sqrt(early[:L]), 1e-12)
    lw = -jnp.exp(wl)
    ti = lax.broadcasted_iota(jnp.int32, (L, L), 0)
    tj = lax.broadcasted_iota(jnp.int32, (L, L), 1)
    cum = _sdot((tj <= ti).astype(f32), lw, _NN)
    w_t, w_prev, w_inv = jnp.exp(cum), jnp.exp(cum - lw), jnp.exp(-cum)
    at, rt, bt, kt = -kk * w_prev, xr * w_t, kk * a * w_inv, k * w_inv
    b4, k4, v4, s4 = stack(bt), stack(kt), stack(xv), stack(s0)

    qt = lax.broadcasted_iota(jnp.int32, (L, wd), 0)
    qi = lax.broadcasted_iota(jnp.int32, (L, wd), 1) % L
    strict, incl = qi < qt, qi <= qt
    ar = jnp.concatenate([at, rt], axis=0)
    prod = _bdot(ar, jnp.concatenate([b4, k4], axis=0), _NT)
    t_ab, t_ak = jnp.where(strict, prod[:L, :wd], 0.0), jnp.where(strict, prod[:L, wd:], 0.0)
    r_b, r_k4 = jnp.where(incl, prod[L:, :wd], 0.0), jnp.where(incl, prod[L:, wd:], 0.0)
    on_s = _adot(ar, s4, _NT)
    on_v = _adot(jnp.concatenate([t_ak, r_k4], axis=0), v4, _NN)
    u = _solve_unit_lower(t_ab, on_s[:L] + on_v[:L])
    y = on_s[L:] + _adot(r_b, stack(u), _NN) + on_v[L:]
    upd = _unstack(_adot(jnp.concatenate([u, xv], axis=0), jnp.concatenate([bt, kt], axis=0), _TN))
    s_new = (s0 + upd) * w_t[L - 1:L, :]

    mu = hsum(y) * (1.0 / n)
    d = y - mu
    var = hsum(d * d) * (1.0 / n)
    yn = d * lax.rsqrt(var + GN_EPS) * ln_w + ln_b
    bonus = early[L:] * xv
    return s_new, (yn + bonus) * g


def _rwkv_chunks(*args):
    gw = RWKV_GROUP * RWKV_HEAD
    res = [_rwkv_chunk(*[t[:, i * gw:(i + 1) * gw] for t in args]) for i in range(args[0].shape[1] // gw)]
    return jnp.concatenate([r[0] for r in res], axis=1), jnp.concatenate([r[1] for r in res], axis=1)


def _rwkv_specs(d, L, rev, nchunk):
    gw = min(RWKV_LANES, d)
    nb = d // gw
    cc = (lambda c: nchunk - 1 - c) if rev else (lambda c: c)
    tok = lambda off: pl.BlockSpec((L, gw), lambda p, c: (cc(c), p + off))
    par = pl.BlockSpec((1, gw), lambda p, c: (0, p))
    return [tok(0), tok(nb), tok(2 * nb), tok(0), tok(0), tok(0)] + [par] * 7, tok(0), par, cc


def _rwkv_fwd(xs_rkv, wpre, apre, gate, params, d):
    s = xs_rkv.shape[0]
    L = RWKV_CHUNK
    nchunk = s // L
    gw = min(RWKV_LANES, d)
    in_specs, tok, par, _ = _rwkv_specs(d, L, False, nchunk)

    def body(*refs):
        ins, (o_ref, ck_ref, st_ref) = refs[:13], refs[13:]

        @pl.when(pl.program_id(1) == 0)
        def _():
            st_ref[...] = jnp.zeros_like(st_ref)

        s0 = st_ref[...]
        ck_ref[0] = s0
        s_new, out = _rwkv_chunks(s0, *[r[...] for r in ins])
        st_ref[...] = s_new
        o_ref[...] = out.astype(o_ref.dtype)

    return pl.pallas_call(
        body, name="rwkv_fwd", grid=(d // gw, nchunk), in_specs=in_specs,
        out_specs=[tok, pl.BlockSpec((1, RWKV_HEAD, gw), lambda p, c: (c, 0, p))],
        out_shape=[jax.ShapeDtypeStruct((s, d), bf16), jax.ShapeDtypeStruct((nchunk, RWKV_HEAD, d), f32)],
        scratch_shapes=[pltpu.VMEM((RWKV_HEAD, gw), f32)], compiler_params=_cp(("parallel", "arbitrary")),
    )(xs_rkv, xs_rkv, xs_rkv, wpre, apre, gate, *params)


def _rwkv_bwd(xs_rkv, wpre, apre, gate, params, ckpt, dout, d):
    s = xs_rkv.shape[0]
    L = RWKV_CHUNK
    nchunk = s // L
    gw = min(RWKV_LANES, d)
    in_specs, tok, par, cc = _rwkv_specs(d, L, True, nchunk)
    in_specs = in_specs + [pl.BlockSpec((1, RWKV_HEAD, gw), lambda p, c: (cc(c), 0, p)), tok]

    def body(*refs):
        ins, ck_ref, do_ref = refs[:13], refs[13], refs[14]
        outs, ds_ref = refs[15:28], refs[28]
        first = pl.program_id(1) == 0

        @pl.when(first)
        def _():
            ds_ref[...] = jnp.zeros_like(ds_ref)

        _, vjp = jax.vjp(_rwkv_chunks, ck_ref[0], *[r[...] for r in ins])
        ct = vjp((ds_ref[...], do_ref[...].astype(f32)))
        ds_ref[...] = ct[0]
        for r, v in zip(outs[:6], ct[1:7]):
            r[...] = v.astype(r.dtype)

        @pl.when(first)
        def _():
            for r in outs[6:]:
                r[...] = jnp.zeros_like(r)

        for r, v in zip(outs[6:], ct[7:]):
            r[...] += v

    return pl.pallas_call(
        body, name="rwkv_bwd", grid=(d // gw, nchunk), in_specs=in_specs, out_specs=[tok] * 6 + [par] * 7,
        out_shape=[jax.ShapeDtypeStruct((s, d), f32)] * 3 + [jax.ShapeDtypeStruct((s, d), bf16)] * 3
        + [jax.ShapeDtypeStruct((1, d), f32)] * 7,
        scratch_shapes=[pltpu.VMEM((RWKV_HEAD, gw), f32)], compiler_params=_cp(("parallel", "arbitrary")),
    )(xs_rkv, xs_rkv, xs_rkv, wpre, apre, gate, *params, ckpt, dout)


def _all_gather(xs, *, name):
    na = len(xs)

    def body(*refs):
        x_refs, o_refs = refs[:na], refs[na:2 * na]
        send_sems, recv_sems, local_sems = refs[2 * na:]
        x, y, c = lax.axis_index("x"), lax.axis_index("y"), lax.axis_index("c")
        me, sibling = (x, y, c), (x, y, 1 - c)
        chips = [(1 - x, y), (x, 1 - y), (1 - x, 1 - y)]

        def blk(a, dev):
            return o_refs[a].at[4 * dev[0] + 2 * dev[1] + dev[2]]

        def copy(a, k, block, to, src=None):
            return pltpu.make_async_remote_copy(
                src_ref=blk(a, block) if src is None else src, dst_ref=blk(a, block),
                send_sem=send_sems.at[a, k], recv_sem=recv_sems.at[a, k], device_id=to, device_id_type=MESH_ID)

        mine = [pltpu.make_async_copy(x_refs[a], blk(a, me), local_sems.at[a]) for a in range(na)]
        for cp in mine:
            cp.start()
        first = []
        for a in range(na):
            first.append(copy(a, 0, me, sibling, src=x_refs[a]))
            first += [copy(a, 1 + j, me, (*chip, c), src=x_refs[a]) for j, chip in enumerate(chips)]
        for cp in first:
            cp.start()
        passed = []
        for a in range(na):
            for j, chip in enumerate(chips):
                copy(a, 1 + j, (*chip, c), me).wait_recv()
                fwd = copy(a, 4 + j, (*chip, c), sibling)
                fwd.start()
                passed.append(fwd)
        for a in range(na):
            copy(a, 0, sibling, me).wait_recv()
            for j, chip in enumerate(chips):
                copy(a, 4 + j, (*chip, 1 - c), me).wait_recv()
        for cp in first + passed:
            cp.wait_send()
        for cp in mine:
            cp.wait()

    anyspec = pl.BlockSpec(memory_space=pl.ANY)
    return pl.pallas_call(
        body, name=name, in_specs=[anyspec] * na, out_specs=[anyspec] * na,
        out_shape=[jax.ShapeDtypeStruct((N_DEV,) + t.shape, t.dtype) for t in xs],
        scratch_shapes=[pltpu.SemaphoreType.DMA((na, 7)), pltpu.SemaphoreType.DMA((na, 7)), pltpu.SemaphoreType.DMA((na,))],
    )(*xs)


def _exchange(xs, *, name):
    na = len(xs)

    def body(*refs):
        x_refs, o_refs = refs[:na], refs[na:2 * na]
        send_sems, recv_sems, local_sems = refs[2 * na:]
        x, y, c = lax.axis_index("x"), lax.axis_index("y"), lax.axis_index("c")
        my_id = 4 * x + 2 * y + c
        mine = [pltpu.make_async_copy(x_refs[a].at[my_id], o_refs[a].at[my_id], local_sems.at[a]) for a in range(na)]
        for cp in mine:
            cp.start()
        copies = []
        for a in range(na):
            for r in range(1, N_DEV):
                px, py, pc = x ^ (r >> 2), y ^ ((r >> 1) & 1), c ^ (r & 1)
                copies.append(pltpu.make_async_remote_copy(
                    src_ref=x_refs[a].at[4 * px + 2 * py + pc], dst_ref=o_refs[a].at[my_id],
                    send_sem=send_sems.at[a, r - 1], recv_sem=recv_sems.at[a, r - 1],
                    device_id=(px, py, pc), device_id_type=MESH_ID))
        for cp in copies:
            cp.start()
        for cp in copies:
            cp.wait_recv()
        for cp in copies:
            cp.wait_send()
        for cp in mine:
            cp.wait()

    anyspec = pl.BlockSpec(memory_space=pl.ANY)
    return pl.pallas_call(
        body, name=name, in_specs=[anyspec] * na, out_specs=[anyspec] * na,
        out_shape=[jax.ShapeDtypeStruct(t.shape, t.dtype) for t in xs],
        scratch_shapes=[pltpu.SemaphoreType.DMA((na, 7)), pltpu.SemaphoreType.DMA((na, 7)), pltpu.SemaphoreType.DMA((na,))],
    )(*xs)


def _peer(r):
    x, y, c = lax.axis_index("x"), lax.axis_index("y"), lax.axis_index("c")
    return x ^ (r >> 2), y ^ ((r >> 1) & 1), c ^ (r & 1)


def _my_id():
    return 4 * lax.axis_index("x") + 2 * lax.axis_index("y") + lax.axis_index("c")


def _split_copies(src_refs, land_refs, send_sems, recv_sems, gather):
    copies = []
    for a, (src, land) in enumerate(zip(src_refs, land_refs)):
        for r in range(1, N_DEV):
            px, py, pc = _peer(r)
            k = a * (N_DEV - 1) + r - 1
            copies.append(pltpu.make_async_remote_copy(
                src_ref=src if gather else src.at[4 * px + 2 * py + pc], dst_ref=land.at[_my_id()],
                send_sem=send_sems.at[k], recv_sem=recv_sems.at[k],
                device_id=(px, py, pc), device_id_type=MESH_ID))
    return copies


_HBM_SPEC = pl.BlockSpec(memory_space=pltpu.HBM)
_SEM_SPEC = pl.BlockSpec(memory_space=pltpu.SEMAPHORE)
_DATAFLOW = pltpu.SideEffectType.DATAFLOW_SIDE_EFFECTING


def _send_start(xs, after, *, gather, name):
    na = len(xs)
    lands = [lax.empty((N_DEV,) + t.shape if gather else t.shape, t.dtype) for t in xs]

    def body(*refs):
        src_refs, land_refs = refs[:na], refs[na:2 * na]
        send_sems, recv_sems = refs[2 * na + 1], refs[2 * na + 2]
        token = refs[-1]
        for cp in _split_copies(src_refs, land_refs, send_sems, recv_sems, gather):
            cp.start()
        token[...] = jnp.zeros_like(token)

    hbm = lambda t: pltpu.with_memory_space_constraint(t, pltpu.HBM)
    res = pl.pallas_call(
        body, name=name,
        out_shape=(pltpu.SemaphoreType.DMA((na * (N_DEV - 1),)), pltpu.SemaphoreType.DMA((na * (N_DEV - 1),)),
                   *[pltpu.HBM(t.shape, t.dtype) for t in xs], *[pltpu.HBM(t.shape, t.dtype) for t in lands],
                   jax.ShapeDtypeStruct((8, LANES), f32)),
        in_specs=[_HBM_SPEC] * (2 * na) + [pl.BlockSpec(memory_space=pl.ANY)],
        out_specs=(_SEM_SPEC, _SEM_SPEC, *[_HBM_SPEC] * (2 * na), pl.BlockSpec(memory_space=pltpu.VMEM)),
        input_output_aliases={i: 2 + i for i in range(2 * na)},
        compiler_params=pltpu.CompilerParams(has_side_effects=_DATAFLOW),
    )(*[hbm(t) for t in xs], *[hbm(t) for t in lands], after)
    return res[0], res[1], list(res[2:2 + na]), list(res[2 + na:2 + 2 * na]), res[-1]


def _send_wait(handle, after, *, gather, name):
    send_sems, recv_sems, srcs, lands, _ = handle
    na = len(srcs)

    def body(*refs):
        src_refs, land_refs = refs[:na], refs[na:2 * na]
        s_sems, r_sems = refs[2 * na], refs[2 * na + 1]
        for cp in _split_copies(src_refs, land_refs, s_sems, r_sems, gather):
            cp.wait_send()
            cp.wait_recv()

    res = pl.pallas_call(
        body, name=name,
        out_shape=tuple(pltpu.HBM(t.shape, t.dtype) for t in srcs + lands),
        in_specs=[_HBM_SPEC] * (2 * na) + [_SEM_SPEC, _SEM_SPEC, pl.BlockSpec(memory_space=pl.ANY)],
        out_specs=tuple([_HBM_SPEC] * (2 * na)), input_output_aliases={i: i for i in range(2 * na)},
        compiler_params=pltpu.CompilerParams(has_side_effects=_DATAFLOW),
    )(*srcs, *lands, send_sems, recv_sems, after)
    return list(res[:na]), list(res[na:])


def _fill_own(land, own):
    return lax.dynamic_update_index_in_dim(land, own, _my_id(), 0)


def _adamw(parts, w, m, v, *, name):
    r, c = w.shape
    tm = _pick(r, (64, 32, 16, 8)) if c > 2048 else _pick(r, (256, 128, 64, 32, 16, 8))
    b1c = 1.0 - ADAM_B1 ** ADAM_STEP
    b2c = 1.0 - ADAM_B2 ** ADAM_STEP

    def body(p_ref, w_ref, m_ref, v_ref, g_ref, d_ref, nm_ref, nv_ref):
        g = p_ref[0].astype(f32)
        for s in range(1, N_DEV):
            g = g + p_ref[s].astype(f32)
        mm = ADAM_B1 * m_ref[...] + (1.0 - ADAM_B1) * g
        vv = ADAM_B2 * v_ref[...] + (1.0 - ADAM_B2) * (g * g)
        g_ref[...] = g
        nm_ref[...] = mm
        nv_ref[...] = vv
        d_ref[...] = -ADAM_LR * ((mm / b1c) / (jnp.sqrt(vv / b2c) + ADAM_EPS) + ADAM_WD * w_ref[...])

    spec = pl.BlockSpec((tm, c), lambda i: (i, 0))
    return pl.pallas_call(
        body, name=name, grid=(r // tm,),
        in_specs=[pl.BlockSpec((N_DEV, tm, c), lambda i: (0, i, 0)), spec, spec, spec], out_specs=[spec] * 4,
        out_shape=[jax.ShapeDtypeStruct((r, c), f32)] * 4, compiler_params=_cp(("parallel",)),
    )(parts, w, m, v)


WEIGHTS = ['norm_mix_pre', 'norm_mix_post', 'norm_ffn_pre', 'norm_ffn_post', 'w_in', 'b_qkv', 'att_sinks', 'mu_shift', 'w0',
           'w2', 'a0', 'a2', 'g2', 'k_k', 'k_a', 'r_k', 'ln_x_w', 'ln_x_b', 'w_att_branch', 'w_rwkv_branch', 'w_out',
           'w_ffn_gate', 'w_ffn_up', 'w_ffn_down']
SHARDED = ['w_in', 'w2', 'a2', 'g2', 'w_att_branch', 'w_rwkv_branch', 'w_out', 'w_ffn_gate', 'w_ffn_up', 'w_ffn_down']
FIRST_WEIGHTS = ['w_in', 'w2', 'a2', 'g2']
BRANCH_WEIGHTS = ['w_att_branch', 'w_rwkv_branch', 'w_out']
FFN_WEIGHTS = ['w_ffn_gate', 'w_ffn_up', 'w_ffn_down']
COL_SHARDED = ['w_in', 'w2', 'a2', 'g2', 'w_ffn_gate', 'w_ffn_up']


def _cols_to_blocks(t):
    r, c = t.shape
    return jnp.moveaxis(t.reshape(r, N_DEV, c // N_DEV), 1, 0)


def _blocks_to_cols(t):
    n, r, c = t.shape
    return jnp.moveaxis(t, 0, 1).reshape(r, n * c)


def _cols_of_blocks(blocks, c0, c1):
    bw = blocks.shape[2]
    parts = []
    for j in range(blocks.shape[0]):
        lo, hi = max(c0, j * bw), min(c1, (j + 1) * bw)
        if lo < hi:
            parts.append(blocks[j, :, lo - j * bw:hi - j * bw])
    return parts[0] if len(parts) == 1 else jnp.concatenate(parts, axis=1)


def _blocks_of_cols(pieces, nblk):
    r = pieces[0].shape[0]
    bw = sum(p.shape[1] for p in pieces) // nblk
    out = lax.empty((nblk, r, bw), pieces[0].dtype)
    c0 = 0
    for p in pieces:
        for j in range(c0 // bw, (c0 + p.shape[1] - 1) // bw + 1):
            lo, hi = max(c0, j * bw), min(c0 + p.shape[1], (j + 1) * bw)
            out = lax.dynamic_update_slice(out, p[None, :, lo - c0:hi - c0], (j, 0, lo - j * bw))
        c0 += p.shape[1]
    return out


def _pad_cols(t, width):
    return t if t.shape[1] == width else jnp.pad(t, ((0, 0), (0, width - t.shape[1])))


def _pad_rows(t, height):
    return t if t.shape[0] == height else jnp.pad(t, ((0, height - t.shape[0]), (0, 0)))


def kernel(x, norm_mix_pre, norm_mix_post, norm_ffn_pre, norm_ffn_post, w_in, b_qkv, att_sinks, mu_shift, w0, w2, a0, a2, g2, k_k, k_a, r_k, ln_x_w, ln_x_b, w_att_branch, w_rwkv_branch, w_out, w_ffn_gate, w_ffn_up, w_ffn_down, loss_target, m_norm_mix_pre, m_norm_mix_post, m_norm_ffn_pre, m_norm_ffn_post, m_w_in, m_b_qkv, m_att_sinks, m_mu_shift, m_w0, m_w2, m_a0, m_a2, m_g2, m_k_k, m_k_a, m_r_k, m_ln_x_w, m_ln_x_b, m_w_att_branch, m_w_rwkv_branch, m_w_out, m_w_ffn_gate, m_w_ffn_up, m_w_ffn_down, v_norm_mix_pre, v_norm_mix_post, v_norm_ffn_pre, v_norm_ffn_post, v_w_in, v_b_qkv, v_att_sinks, v_mu_shift, v_w0, v_w2, v_a0, v_a2, v_g2, v_k_k, v_k_a, v_r_k, v_ln_x_w, v_ln_x_b, v_w_att_branch, v_w_rwkv_branch, v_w_out, v_w_ffn_gate, v_w_ffn_up, v_w_ffn_down):
    loc = dict(locals())
    wts = {n: loc[n] for n in WEIGHTS}
    mom_m = {n: loc["m_" + n] for n in WEIGHTS}
    mom_v = {n: loc["v_" + n] for n in WEIGHTS}

    xt = x[0]
    tgt = loss_target[0]
    s, d = xt.shape
    qkvd = b_qkv.shape[1]
    qd = w_att_branch.shape[1] * N_DEV
    kvd = (qkvd - qd) // 2
    dd, da, dg = w2.shape[1], a2.shape[1], g2.shape[1]
    ddp, dap, dgp = _round_up(dd, LANES), _round_up(da, LANES), _round_up(dg, LANES)
    lrp = dgp + ddp + dap
    hkv = kvd // ATT_HEAD
    tm_row = 128

    shards = {n: wts[n][0].astype(bf16) for n in SHARDED}
    gw = dict(zip(FIRST_WEIGHTS, _all_gather([shards[n] for n in FIRST_WEIGHTS], name="gather_w_in")))
    h_branch = _send_start([shards[n] for n in BRANCH_WEIGHTS], gw['w2'], gather=True, name="gather_branch_start")
    h_ffn = _send_start([shards[n] for n in FFN_WEIGHTS], h_branch[-1], gather=True, name="gather_ffn_start")
    started = h_ffn[-1][0:1, 0:1]
    wcols = functools.partial(_cols_of_blocks, gw['w_in'])
    lr0 = qkvd + 3 * d
    w_qkv = wcols(0, qkvd)
    w_rkv = wcols(qkvd, lr0)
    w_lr = jnp.concatenate([_pad_cols(wcols(lr0 + dd + da, lr0 + dd + da + dg), dgp), _pad_cols(wcols(lr0, lr0 + dd), ddp),
                            _pad_cols(wcols(lr0 + dd, lr0 + dd + da), dap)], axis=1)
    w_gates = wcols(lr0 + dd + da + dg, lr0 + dd + da + dg + 2 * d)
    w2_f = _pad_rows(_blocks_to_cols(gw['w2']), ddp)
    a2_f = _pad_rows(_blocks_to_cols(gw['a2']), dap)
    g2_f = _pad_rows(_blocks_to_cols(gw['g2']), dgp)

    mu = mu_shift
    mu_rkv = mu[:, :3 * d]
    mu_lr = jnp.concatenate([_pad_cols(mu[:, 3 * d + dd + da:], dgp), _pad_cols(mu[:, 3 * d:3 * d + dd], ddp),
                             _pad_cols(mu[:, 3 * d + dd:3 * d + dd + da], dap)], axis=1)
    pos = jnp.arange(s, dtype=f32)
    inv_freq = ROPE_THETA ** (-jnp.arange(0, ATT_HEAD, 2, dtype=f32) / ATT_HEAD)
    ang = pos[:, None] * inv_freq[None, :]
    cos, sin = jnp.cos(ang), jnp.sin(ang)
    sinks4 = att_sinks.reshape(hkv, ATT_GROUP, 1, 1)
    rw_params = [w0, a0, k_k, k_a, ln_x_w, ln_x_b, r_k.reshape(1, d)]

    (h1,), _ = _rowwise(lambda xv, g: ((_rms(xv, g),), ()), [xt], [norm_mix_pre + started], [(d, bf16)], [], tm=tm_row,
                        name="norm_mix_pre")
    qkv = _mm(h1, w_qkv, bias=b_qkv, name="proj_qkv")
    p_rkv = _mm(h1, w_rkv, name="proj_rkv")
    p_lr = _mm(h1, w_lr, name="proj_lr")
    p_gates = _mm(h1, w_gates, name="proj_gates")

    o_att = _att_fwd(qkv, sinks4, cos, sin, qd, kvd)

    xs_rkv = _shift_fwd(p_rkv, mu_rkv, name="shift_rkv")
    xs_lr = _shift_fwd(p_lr, mu_lr, name="shift_lr")

    def lr_act(v):
        return jnp.concatenate([_sigmoid(v[:, :dgp]), jnp.tanh(v[:, dgp:dgp + ddp]), v[:, dgp + ddp:]], axis=-1)

    (act_lr,), _ = _rowwise(lambda v: ((lr_act(v),), ()), [xs_lr], [], [(lrp, bf16)], [], tm=256, name="lowrank_act")
    gate_r = _mm(act_lr, g2_f, a_col0=0, a_cols=dgp, name="lowrank_g")
    wpre = _mm(act_lr, w2_f, a_col0=dgp, a_cols=ddp, name="lowrank_w")
    apre = _mm(act_lr, a2_f, a_col0=dgp + ddp, a_cols=dap, name="lowrank_a")
    o_rwkv, ckpt = _rwkv_fwd(xs_rkv, wpre, apre, gate_r, rw_params, d)

    srcs, lands = _send_wait(h_branch, o_rwkv, gather=True, name="gather_branch_wait")
    w_ab, w_rb, w_o = [_fill_own(l, t).reshape(-1, d) for l, t in zip(lands, srcs)]
    oa = _mm(o_att, w_ab, name="att_branch")
    ob = _mm(o_rwkv, w_rb, name="rwkv_branch")

    def merge(oav, obv, gv):
        return _sigmoid(gv[:, :d]) * oav + _sigmoid(gv[:, d:]) * obv

    (merged,), _ = _rowwise(lambda oav, obv, gv: ((merge(oav, obv, gv),), ()), [oa, ob, p_gates], [], [(d, bf16)], [],
                            tm=tm_row, name="merge")
    mo = _mm(merged, w_o, name="out_proj")

    def mid(xv, mov, g_post, g_pre):
        x2 = xv + _rms(mov, g_post)
        return (x2, _rms(x2, g_pre)), ()

    (x2, h2), _ = _rowwise(mid, [xt, mo], [norm_mix_post, norm_ffn_pre], [(d, f32), (d, bf16)], [], tm=tm_row, name="mid_norms")
    srcs, lands = _send_wait(h_ffn, h2, gather=True, name="gather_ffn_wait")
    w_fg, w_fu, w_fd = [_fill_own(l, t) for l, t in zip(lands, srcs)]
    fgate = _mm(h2, w_fg, o_blk=True, name="ffn_gate")
    fup = _mm(h2, w_fu, o_blk=True, name="ffn_up")
    fb = fgate.shape[2]

    def swiglu(gv, uv):
        return gv * _sigmoid(gv) * uv

    (act,), _ = _rowwise(lambda gv, uv: ((swiglu(gv, uv),), ()), [fgate.reshape(N_DEV * s, fb), fup.reshape(N_DEV * s, fb)], [],
                         [(fb, bf16)], [], tm=512, name="swiglu")
    act3 = act.reshape(N_DEV, s, fb)
    fo = _mm(act3, w_fd, name="ffn_down")

    def head(fv, x2v, tv, g_post):
        yv, vjp = jax.vjp(lambda f_, g_: _rms(f_, g_), fv, g_post)
        err = x2v + yv - tv
        dy = err * (1.0 / d)
        df, dgp_ = vjp(dy)
        lossp = 0.5 * jnp.sum(jnp.sum(err * err, axis=-1, keepdims=True) * (1.0 / d), axis=0, keepdims=True)
        return (df, dy), (dgp_, lossp)

    (dfo, dy), (g_norm_ffn_post, loss_part) = _rowwise(head, [fo, x2, tgt], [norm_ffn_post], [(d, bf16), (d, f32)],
                                                       [(1, d), (1, 1)], tm=tm_row, name="loss_head")
    dact = _mm(dfo, w_fd, tb=True, o_blk=True, out_dtype=bf16, name="d_act")
    g_fd = _mm(act3, dfo, ta=True, o_blk=True, out_dtype=bf16, name="g_ffn_down")

    def swiglu_bwd(gv, uv, dav):
        _, vjp = jax.vjp(swiglu, gv, uv)
        return vjp(dav.astype(f32)), ()

    (dgate, dup), _ = _rowwise(swiglu_bwd, [fgate.reshape(N_DEV * s, fb), fup.reshape(N_DEV * s, fb), dact.reshape(N_DEV * s, fb)],
                               [], [(fb, bf16), (fb, bf16)], [], tm=512, name="swiglu_bwd")
    dgate3, dup3 = dgate.reshape(N_DEV, s, fb), dup.reshape(N_DEV, s, fb)
    g_fg = _mm(h2, dgate3, ta=True, o_blk=True, out_dtype=bf16, name="g_ffn_gate")
    g_fu = _mm(h2, dup3, ta=True, o_blk=True, out_dtype=bf16, name="g_ffn_up")
    h_g_ffn = _send_start([g_fg, g_fu, g_fd], loss_part, gather=False, name="exchange_ffn_start")
    started = h_g_ffn[-1][0:1, 0:1]
    dh2 = _mm(dgate3, w_fg, tb=True, name="d_h2_gate")
    dh2 = _mm(dup3, w_fu, tb=True, add=dh2, name="d_h2_up")

    def mid_bwd(xv, mov, dyv, dh2v, g_post, g_pre):
        x2v = xv + _rms(mov, g_post)
        _, vjp2 = jax.vjp(lambda t, g_: _rms(t, g_), x2v, g_pre)
        dx2a, dg_pre = vjp2(dh2v)
        dx2 = dyv + dx2a
        _, vjp1 = jax.vjp(lambda t, g_: _rms(t, g_), mov, g_post)
        dmo, dg_post = vjp1(dx2)
        return (dx2, dmo), (dg_pre, dg_post)

    (dx2, dmo), (g_norm_ffn_pre, g_norm_mix_post) = _rowwise(
        mid_bwd, [xt, mo, dy, dh2], [norm_mix_post, norm_ffn_pre + started], [(d, f32), (d, bf16)], [(1, d), (1, d)], tm=tm_row,
        name="mid_norms_bwd")
    dmerged = _mm(dmo, w_o, tb=True, name="d_merged")
    g_wo = _mm(merged, dmo, ta=True, out_dtype=bf16, name="g_w_out")

    def merge_bwd(oav, obv, gv, dmv):
        _, vjp = jax.vjp(merge, oav, obv, gv)
        return vjp(dmv), ()

    (doa, dob, dgates), _ = _rowwise(merge_bwd, [oa, ob, p_gates, dmerged], [], [(d, bf16), (d, bf16), (2 * d, bf16)], [],
                                     tm=tm_row, name="merge_bwd")
    do_att = _mm(doa, w_ab, tb=True, out_dtype=bf16, name="d_o_att")
    g_wab = _mm(o_att, doa, ta=True, out_dtype=bf16, name="g_w_att_branch")
    do_rwkv = _mm(dob, w_rb, tb=True, out_dtype=bf16, name="d_o_rwkv")
    g_wrb = _mm(o_rwkv, dob, ta=True, out_dtype=bf16, name="g_w_rwkv_branch")

    h_g_branch = _send_start([g_wab.reshape(N_DEV, -1, d), g_wrb.reshape(N_DEV, -1, d), g_wo.reshape(N_DEV, -1, d)], loss_part,
                             gather=False, name="exchange_branch_start")
    started = h_g_branch[-1][0:1, 0:1].reshape(1, 1, 1, 1)
    dq, dkc, dkp, dvc, dvp, dsinks = _att_bwd(qkv, sinks4 + started, cos, sin, do_att, qd, kvd)
    dqkv, g_b_qkv = _att_dqkv(dq, dkc, dkp, dvc, dvp)

    rb = _rwkv_bwd(xs_rkv, wpre, apre, gate_r, rw_params, ckpt, do_rwkv, d)
    dxs_r, dxs_k, dxs_v, dwpre, dapre, dgate_r = rb[:6]
    g_w0, g_a0, g_kk, g_ka, g_lnw, g_lnb, g_rk = rb[6:]
    g_g2 = _mm(act_lr, dgate_r, ta=True, name="g_g2")[:dgp]
    g_lr_rows = None
    d_sg = _mm(dgate_r, g2_f, tb=True, name="d_lowrank_g")
    d_tw = _mm(dwpre, w2_f, tb=True, name="d_lowrank_w")
    d_xa = _mm(dapre, a2_f, tb=True, name="d_lowrank_a")
    g_w2 = _mm(act_lr, dwpre, ta=True, name="g_w2")[dgp:dgp + ddp]
    g_a2 = _mm(act_lr, dapre, ta=True, name="g_a2")[dgp + ddp:]

    def lr_bwd(v, dsv, dtv, dav):
        _, vjp = jax.vjp(lr_act, v)
        return vjp(jnp.concatenate([dsv, dtv, dav], axis=-1)), ()

    (dxs_lr,), _ = _rowwise(lr_bwd, [xs_lr, d_sg, d_tw, d_xa], [], [(lrp, f32)], [], tm=256, name="lowrank_act_bwd")

    d_r, gmu_r = _shift_bwd(p_rkv, 0, dxs_r, mu_rkv[:, :d], name="shift_bwd_r")
    d_k, gmu_k = _shift_bwd(p_rkv, d, dxs_k, mu_rkv[:, d:2 * d], name="shift_bwd_k")
    d_v, gmu_v = _shift_bwd(p_rkv, 2 * d, dxs_v, mu_rkv[:, 2 * d:], name="shift_bwd_v")
    d_lr, gmu_lr = _shift_bwd(p_lr, 0, dxs_lr, mu_lr, name="shift_bwd_lr")

    gw_qkv = _mm(h1, dqkv, ta=True, out_dtype=bf16, name="g_w_qkv")
    gw_r = _mm(h1, d_r, ta=True, out_dtype=bf16, name="g_w_r")
    gw_k = _mm(h1, d_k, ta=True, out_dtype=bf16, name="g_w_k")
    gw_v = _mm(h1, d_v, ta=True, out_dtype=bf16, name="g_w_v")
    gw_lr = _mm(h1, d_lr, ta=True, out_dtype=bf16, name="g_w_lr")
    gw_gates = _mm(h1, dgates, ta=True, out_dtype=bf16, name="g_w_gates")
    g_w_in = _blocks_of_cols([gw_qkv, gw_r, gw_k, gw_v, gw_lr[:, dgp:dgp + dd], gw_lr[:, dgp + ddp:dgp + ddp + da],
                              gw_lr[:, :dg], gw_gates], N_DEV)
    send = {'w_in': g_w_in, 'w2': _cols_to_blocks(g_w2[:dd].astype(bf16)),
            'a2': _cols_to_blocks(g_a2[:da].astype(bf16)), 'g2': _cols_to_blocks(g_g2[:dg].astype(bf16))}
    h_g_in = _send_start([send[n] for n in FIRST_WEIGHTS], loss_part, gather=False, name="exchange_w_in_start")
    started = h_g_in[-1][0:1, 0:1]

    dh1 = _mm(dqkv, w_qkv, tb=True, bias=jnp.zeros((1, d), f32) + started, name="d_h1_qkv")
    dh1 = _mm(d_r, w_rkv[:, :d], tb=True, add=dh1, name="d_h1_r")
    dh1 = _mm(d_k, w_rkv[:, d:2 * d], tb=True, add=dh1, name="d_h1_k")
    dh1 = _mm(d_v, w_rkv[:, 2 * d:], tb=True, add=dh1, name="d_h1_v")
    dh1 = _mm(d_lr, w_lr, tb=True, add=dh1, name="d_h1_lr")
    dh1 = _mm(dgates, w_gates, tb=True, add=dh1, name="d_h1_gates")

    def pre_bwd(xv, dx2v, dh1v, g_pre):
        _, vjp = jax.vjp(lambda t, g_: _rms(t, g_), xv, g_pre)
        dxa, dg_pre = vjp(dh1v)
        return (dx2v + dxa,), (dg_pre,)

    (grad_x,), (g_norm_mix_pre,) = _rowwise(pre_bwd, [xt, dx2, dh1], [norm_mix_pre], [(d, f32)], [(1, d)], tm=tm_row,
                                            name="norm_mix_pre_bwd")

    g_mu = jnp.concatenate([gmu_r, gmu_k, gmu_v, gmu_lr[:, dgp:dgp + dd], gmu_lr[:, dgp + ddp:dgp + ddp + da], gmu_lr[:, :dg]],
                           axis=1)
    small = {'norm_mix_pre': g_norm_mix_pre, 'norm_mix_post': g_norm_mix_post, 'norm_ffn_pre': g_norm_ffn_pre,
             'norm_ffn_post': g_norm_ffn_post, 'b_qkv': g_b_qkv, 'att_sinks': dsinks.reshape(1, -1), 'mu_shift': g_mu,
             'w0': g_w0, 'a0': g_a0, 'k_k': g_kk, 'k_a': g_ka, 'r_k': g_rk, 'ln_x_w': g_lnw, 'ln_x_b': g_lnb}
    small_names = [n for n in WEIGHTS if n not in SHARDED]
    sizes = [wts[n].size for n in small_names]
    total = sum(sizes)
    rows_small = _round_up(_round_up(total, LANES) // LANES, 8)

    def pack(parts):
        flat = jnp.concatenate([p.reshape(-1).astype(f32) for p in parts])
        return jnp.pad(flat, (0, rows_small * LANES - total)).reshape(rows_small, LANES)

    h_small = _send_start([pack([small[n] for n in small_names])], loss_part, gather=True, name="gather_small_start")

    out_g, out_d, out_m, out_v = {}, {}, {}, {}
    after = h_small[-1]

    def update_group(names, handle, nm, after):
        srcs, lands = _send_wait(handle, after, gather=False, name=nm)
        for n, t, l in zip(names, srcs, lands):
            parts = _fill_own(l, lax.dynamic_index_in_dim(t, _my_id(), 0, keepdims=False))
            res = _adamw(parts, wts[n][0], mom_m[n][0], mom_v[n][0], name="adamw_" + n)
            for dst, src in zip((out_g, out_d, out_m, out_v), res):
                dst[n] = src.reshape(wts[n].shape)
        return res[0]

    after = update_group(FFN_WEIGHTS, h_g_ffn, "exchange_ffn_wait", after)
    after = update_group(BRANCH_WEIGHTS, h_g_branch, "exchange_branch_wait", after)
    srcs, lands = _send_wait(h_small, after, gather=True, name="gather_small_wait")
    sg, sd, sm, sv = _adamw(_fill_own(lands[0], srcs[0]), pack([wts[n] for n in small_names]),
                            pack([mom_m[n] for n in small_names]), pack([mom_v[n] for n in small_names]), name="adamw_small")
    off = 0
    for n, sz in zip(small_names, sizes):
        for dst, src in ((out_g, sg), (out_d, sd), (out_m, sm), (out_v, sv)):
            dst[n] = src.reshape(-1)[off:off + sz].reshape(wts[n].shape)
        off += sz
    update_group(FIRST_WEIGHTS, h_g_in, "exchange_w_in_wait", sg)

    loss = lax.psum(loss_part[0, 0], MESH_AXES)
    return (loss, grad_x[None], *[out_g[n] for n in WEIGHTS], *[out_d[n] for n in WEIGHTS],
            *[out_m[n] for n in WEIGHTS], *[out_v[n] for n in WEIGHTS])
```

```python
import functools
import math

import jax
import jax.numpy as jnp
from jax import lax
from jax.experimental import pallas as pl
from jax.experimental.pallas import tpu as pltpu

f32 = jnp.float32
bf16 = jnp.bfloat16

N_DEV = 8
MESH_AXES = ("x", "y", "c")
LANES = 128
ATT_HEAD = 128
ATT_GROUP = 4
ATT_BLOCK = 128
ROPE_THETA = 10000.0
NEG_INF = -1e30
RWKV_HEAD = 64
RWKV_GROUP = 4
RWKV_CHUNK = 64
RWKV_LANES = 8 * RWKV_GROUP * RWKV_HEAD
RMS_EPS = 1e-6
GN_EPS = 64e-5
ADAM_LR, ADAM_B1, ADAM_B2, ADAM_EPS, ADAM_WD, ADAM_STEP = 0.001, 0.9, 0.999, 1e-08, 0.01, 10
VMEM_LIMIT = 56 * 1024 * 1024
MM_FULL_K = 4096
MM_VMEM_BUDGET = 40 * 1024 * 1024
SCAN_PREC = lax.Precision.HIGH
BULK_PREC = lax.Precision.HIGH
APPLY_PREC = lax.Precision.HIGH
CHAIN_PREC = lax.Precision.DEFAULT
MESH_ID = pl.DeviceIdType.MESH


def _cp(sem):
    return pltpu.CompilerParams(dimension_semantics=sem, vmem_limit_bytes=VMEM_LIMIT)


def _pick(n, cands):
    for c in cands:
        if n % c == 0:
            return c
    return n


def _round_up(n, m):
    return (n + m - 1) // m * m


def _mm(a, b, *, name, ta=False, tb=False, o_blk=False, out_dtype=f32, bias=None, add=None, a_col0=0, a_cols=None):
    a3 = a if a.ndim == 3 else a[None]
    b3 = b if b.ndim == 3 else b[None]
    ja, jb = a3.shape[0], b3.shape[0]
    nj = max(ja, jb)
    if ta:
        kdim, m = a3.shape[1:]
    else:
        m, kdim = a3.shape[1:]
    if a_cols is not None:
        kdim = a_cols
    if tb:
        n, kb = b3.shape[1:]
    else:
        kb, n = b3.shape[1:]
    assert kdim == kb, (name, a3.shape, b3.shape)
    reduce_j = nj > 1 and not o_blk
    tm = _pick(m, (1024, 512, 256, 128) if reduce_j else (512, 256, 128))
    tn = _pick(n, (1024, 512, 256, 128))
    tk = kdim if kdim <= MM_FULL_K else _pick(kdim, (2048, 1024, 512, 256, 128))
    osz = jnp.dtype(out_dtype).itemsize + (4 if add is not None else 0)
    vmem = lambda tn_, tk_: 2 * (a3.dtype.itemsize * tm * tk_ + b3.dtype.itemsize * tk_ * tn_ + osz * tm * tn_) + 8 * tm * tn_
    while vmem(tn, tk) > MM_VMEM_BUDGET and tn % 256 == 0:
        tn //= 2
    while vmem(tn, tk) > MM_VMEM_BUDGET and tk % 256 == 0:
        tk //= 2
    assert a_col0 % tk == 0
    kc0 = a_col0 // tk
    nm, nn, nk = m // tm, n // tn, kdim // tk
    if reduce_j:
        grid = (nm, nn, nj, nk)
        unpack = lambda i0, i1, i2, i3: (i2, i0, i1, i3)
        sem = ("parallel", "parallel", "arbitrary", "arbitrary")
    else:
        grid = (nj, nm, nn, nk)
        unpack = lambda i0, i1, i2, i3: (i0, i1, i2, i3)
        sem = ("parallel", "parallel", "parallel", "arbitrary")

    def a_map(*g):
        j, mi, ni, ki = unpack(*g)
        jj = j if ja > 1 else 0
        return (jj, ki, mi) if ta else (jj, mi, ki + kc0)

    def b_map(*g):
        j, mi, ni, ki = unpack(*g)
        jj = j if jb > 1 else 0
        return (jj, ni, ki) if tb else (jj, ki, ni)

    def o_map(*g):
        j, mi, ni, ki = unpack(*g)
        return (j if o_blk else 0, mi, ni)

    in_specs = [pl.BlockSpec((1, tk, tm) if ta else (1, tm, tk), a_map),
                pl.BlockSpec((1, tn, tk) if tb else (1, tk, tn), b_map)]
    args = [a3, b3]
    if bias is not None:
        in_specs.append(pl.BlockSpec((1, tn), lambda *g: (0, unpack(*g)[2])))
        args.append(bias)
    if add is not None:
        add3 = add if add.ndim == 3 else add[None]
        in_specs.append(pl.BlockSpec((1, tm, tn), o_map))
        args.append(add3)
    has_bias, has_add = bias is not None, add is not None
    single = nk == 1 and not reduce_j
    dims = (((0 if ta else 1,), (1 if tb else 0,)), ((), ()))

    def body(*refs):
        a_ref, b_ref = refs[0], refs[1]
        o_ref, acc_ref = (refs[-1], None) if single else (refs[-2], refs[-1])
        j, mi, ni, ki = unpack(pl.program_id(0), pl.program_id(1), pl.program_id(2), pl.program_id(3))
        first = ki == 0
        last = ki == nk - 1
        if reduce_j:
            first = jnp.logical_and(first, j == 0)
            last = jnp.logical_and(last, j == nj - 1)

        prod = lax.dot_general(a_ref[0].astype(bf16), b_ref[0].astype(bf16), dims, preferred_element_type=f32)

        def finish(r):
            idx = 2
            if has_bias:
                r = r + refs[idx][...]
                idx += 1
            if has_add:
                r = r + refs[idx][0].astype(f32)
            o_ref[0] = r.astype(out_dtype)

        if single:
            finish(prod)
        else:
            @pl.when(first)
            def _():
                acc_ref[...] = prod

            @pl.when(jnp.logical_not(first))
            def _():
                acc_ref[...] += prod

            @pl.when(last)
            def _():
                finish(acc_ref[...])

    jo = nj if o_blk else 1
    out = pl.pallas_call(
        body, name=name, grid=grid, in_specs=in_specs, out_specs=pl.BlockSpec((1, tm, tn), o_map),
        out_shape=jax.ShapeDtypeStruct((jo, m, n), out_dtype), scratch_shapes=[] if single else [pltpu.VMEM((tm, tn), f32)],
        compiler_params=_cp(sem),
    )(*args)
    return out if o_blk else out[0]


def _rowwise(fn, rows, consts, out_rows, out_accs, *, tm, name):
    nrow = rows[0].shape[0]
    assert all(r.shape[0] == nrow for r in rows) and nrow % tm == 0
    nr, nc, no, na = len(rows), len(consts), len(out_rows), len(out_accs)

    def body(*refs):
        rin, cin = refs[:nr], refs[nr:nr + nc]
        orow, oacc = refs[nr + nc:nr + nc + no], refs[nr + nc + no:]
        ro, ao = fn(*[r[...] for r in rin], *[c[...] for c in cin])
        for r, v in zip(orow, ro):
            r[...] = v.astype(r.dtype)
        if na:
            @pl.when(pl.program_id(0) == 0)
            def _():
                for r in oacc:
                    r[...] = jnp.zeros_like(r)
            for r, v in zip(oacc, ao):
                r[...] += v

    in_specs = [pl.BlockSpec((tm, r.shape[1]), lambda i: (i, 0)) for r in rows]
    in_specs += [pl.BlockSpec(c.shape, lambda i, nd=c.ndim: (0,) * nd) for c in consts]
    out_specs = [pl.BlockSpec((tm, w), lambda i: (i, 0)) for w, _ in out_rows]
    out_specs += [pl.BlockSpec(s, lambda i, nd=len(s): (0,) * nd) for s in out_accs]
    out_shape = [jax.ShapeDtypeStruct((nrow, w), dt) for w, dt in out_rows]
    out_shape += [jax.ShapeDtypeStruct(s, f32) for s in out_accs]
    res = pl.pallas_call(body, name=name, grid=(nrow // tm,), in_specs=in_specs, out_specs=out_specs, out_shape=out_shape,
                         compiler_params=_cp(("arbitrary",)))(*rows, *consts)
    return res[:no], res[no:]


def _rms(x, g):
    return x * lax.rsqrt(jnp.mean(x * x, axis=-1, keepdims=True) + RMS_EPS) * g


def _sigmoid(x):
    return 1.0 / (1.0 + jnp.exp(-x))


def _colsum(v):
    return jnp.sum(v, axis=0, keepdims=True)


def _prev_rows(xv, edge_row, has_edge):
    rolled = pltpu.roll(xv, 1, 0)
    row = lax.broadcasted_iota(jnp.int32, xv.shape, 0)
    edge = jnp.where(has_edge, edge_row, jnp.zeros_like(edge_row))
    return jnp.where(row == 0, edge, rolled)


def _next_rows(xv, edge_row, has_edge):
    tm = xv.shape[0]
    rolled = pltpu.roll(xv, tm - 1, 0)
    row = lax.broadcasted_iota(jnp.int32, xv.shape, 0)
    edge = jnp.where(has_edge, edge_row, jnp.zeros_like(edge_row))
    return jnp.where(row == tm - 1, edge, rolled)


def _shift_fwd(x, mu, *, name, tm=256):
    s, c = x.shape
    tc = _pick(c, (2048, 1024, 512, 256, 128))
    g8 = tm // 8

    def body(x_ref, xp_ref, mu_ref, o_ref):
        i = pl.program_id(1)
        xv = x_ref[...]
        prev = _prev_rows(xv, xp_ref[7:8, :], i > 0)
        o_ref[...] = xv + (prev - xv) * mu_ref[...]

    return pl.pallas_call(
        body, name=name, grid=(c // tc, s // tm),
        in_specs=[pl.BlockSpec((tm, tc), lambda j, i: (i, j)),
                  pl.BlockSpec((8, tc), lambda j, i: (jnp.maximum(i * g8 - 1, 0), j)),
                  pl.BlockSpec((1, tc), lambda j, i: (0, j))],
        out_specs=pl.BlockSpec((tm, tc), lambda j, i: (i, j)),
        out_shape=jax.ShapeDtypeStruct((s, c), f32), compiler_params=_cp(("parallel", "arbitrary")),
    )(x, x, mu)


def _shift_bwd(x, col0, dxs, mu, *, name, out_dtype=bf16, tm=256):
    s, c = dxs.shape
    tc = _pick(c, (2048, 1024, 512, 256, 128))
    assert col0 % tc == 0
    jc0 = col0 // tc
    g8 = tm // 8
    ni = s // tm
    last8 = s // 8 - 1

    def body(x_ref, xp_ref, d_ref, dn_ref, mu_ref, dx_ref, dmu_ref):
        i = pl.program_id(1)
        xv = x_ref[...]
        prev = _prev_rows(xv, xp_ref[7:8, :], i > 0)
        dv = d_ref[...].astype(f32)
        dnext = _next_rows(dv, dn_ref[0:1, :].astype(f32), i < ni - 1)
        muv = mu_ref[...]
        dx_ref[...] = (dv * (1.0 - muv) + dnext * muv).astype(dx_ref.dtype)

        @pl.when(i == 0)
        def _():
            dmu_ref[...] = jnp.zeros_like(dmu_ref)

        dmu_ref[...] += _colsum(dv * (prev - xv))

    return pl.pallas_call(
        body, name=name, grid=(c // tc, ni),
        in_specs=[pl.BlockSpec((tm, tc), lambda j, i: (i, j + jc0)),
                  pl.BlockSpec((8, tc), lambda j, i: (jnp.maximum(i * g8 - 1, 0), j + jc0)),
                  pl.BlockSpec((tm, tc), lambda j, i: (i, j)),
                  pl.BlockSpec((8, tc), lambda j, i: (jnp.minimum((i + 1) * g8, last8), j)),
                  pl.BlockSpec((1, tc), lambda j, i: (0, j))],
        out_specs=[pl.BlockSpec((tm, tc), lambda j, i: (i, j)), pl.BlockSpec((1, tc), lambda j, i: (0, j))],
        out_shape=[jax.ShapeDtypeStruct((s, c), out_dtype), jax.ShapeDtypeStruct((1, c), f32)],
        compiler_params=_cp(("parallel", "arbitrary")),
    )(x, x, dxs, dxs, mu)


def _rope(t, cos, sin):
    h = ATT_HEAD // 2
    t1, t2 = t[:, :h], t[:, h:]
    return jnp.concatenate([t1 * cos - t2 * sin, t2 * cos + t1 * sin], axis=-1)


def _att_block(q, kc, kp, vc, vp, sinks, cos_c, sin_c, cos_p, sin_p, first_block):
    blk = ATT_BLOCK
    kk = jnp.concatenate([_rope(kp, cos_p, sin_p), _rope(kc, cos_c, sin_c)], axis=0).astype(bf16)
    vv = jnp.concatenate([vp, vc], axis=0).astype(bf16)
    qi = lax.broadcasted_iota(jnp.int32, (blk, 2 * blk), 0)
    kj = lax.broadcasted_iota(jnp.int32, (blk, 2 * blk), 1)
    rel = qi + blk - kj
    mask = (rel >= 0) & (rel < blk) & jnp.logical_or(kj >= blk, jnp.logical_not(first_block))
    outs = []
    for g in range(ATT_GROUP):
        qg = _rope(q[:, g * ATT_HEAD:(g + 1) * ATT_HEAD], cos_c, sin_c).astype(bf16)
        sc = lax.dot_general(qg, kk, (((1,), (1,)), ((), ())), preferred_element_type=f32) * (ATT_HEAD ** -0.5)
        sc = jnp.where(mask, sc, NEG_INF)
        sink = sinks[g]
        m = jnp.maximum(jnp.max(sc, axis=-1, keepdims=True), sink)
        p = jnp.exp(sc - m)
        probs = p / (jnp.sum(p, axis=-1, keepdims=True) + jnp.exp(sink - m))
        outs.append(jnp.dot(probs.astype(bf16), vv, preferred_element_type=f32))
    return jnp.concatenate(outs, axis=-1)


def _att_specs(qd, kvd):
    gw = ATT_GROUP * ATT_HEAD
    kb0, vb0 = qd // ATT_HEAD, (qd + kvd) // ATT_HEAD
    prev = lambda n: jnp.maximum(n - 1, 0)
    half = ATT_HEAD // 2
    return [
        pl.BlockSpec((ATT_BLOCK, gw), lambda h, n: (n, h)),
        pl.BlockSpec((ATT_BLOCK, ATT_HEAD), lambda h, n: (n, kb0 + h)),
        pl.BlockSpec((ATT_BLOCK, ATT_HEAD), lambda h, n: (prev(n), kb0 + h)),
        pl.BlockSpec((ATT_BLOCK, ATT_HEAD), lambda h, n: (n, vb0 + h)),
        pl.BlockSpec((ATT_BLOCK, ATT_HEAD), lambda h, n: (prev(n), vb0 + h)),
        pl.BlockSpec((1, ATT_GROUP, 1, 1), lambda h, n: (h, 0, 0, 0)),
        pl.BlockSpec((ATT_BLOCK, half), lambda h, n: (n, 0)),
        pl.BlockSpec((ATT_BLOCK, half), lambda h, n: (n, 0)),
        pl.BlockSpec((ATT_BLOCK, half), lambda h, n: (prev(n), 0)),
        pl.BlockSpec((ATT_BLOCK, half), lambda h, n: (prev(n), 0)),
    ]


def _att_fwd(qkv, sinks4, cos, sin, qd, kvd):
    s = qkv.shape[0]
    hkv = kvd // ATT_HEAD
    gw = ATT_GROUP * ATT_HEAD

    def body(q_ref, kc_ref, kp_ref, vc_ref, vp_ref, sk_ref, cc_ref, sc_ref, cp_ref, sp_ref, o_ref):
        sinks = [sk_ref[0, g] for g in range(ATT_GROUP)]
        o = _att_block(q_ref[...], kc_ref[...], kp_ref[...], vc_ref[...], vp_ref[...], sinks,
                       cc_ref[...], sc_ref[...], cp_ref[...], sp_ref[...], pl.program_id(1) == 0)
        o_ref[...] = o.astype(o_ref.dtype)

    return pl.pallas_call(
        body, name="att_fwd", grid=(hkv, s // ATT_BLOCK), in_specs=_att_specs(qd, kvd),
        out_specs=pl.BlockSpec((ATT_BLOCK, gw), lambda h, n: (n, h)),
        out_shape=jax.ShapeDtypeStruct((s, qd), bf16), compiler_params=_cp(("parallel", "arbitrary")),
    )(qkv, qkv, qkv, qkv, qkv, sinks4, cos, sin, cos, sin)


def _att_bwd(qkv, sinks4, cos, sin, do, qd, kvd):
    s = qkv.shape[0]
    hkv = kvd // ATT_HEAD
    gw = ATT_GROUP * ATT_HEAD

    def body(q_ref, kc_ref, kp_ref, vc_ref, vp_ref, sk_ref, cc_ref, sc_ref, cp_ref, sp_ref, do_ref,
             dq_ref, dkc_ref, dkp_ref, dvc_ref, dvp_ref, dsk_ref):
        n = pl.program_id(1)
        cc, sc, cp, sp = cc_ref[...], sc_ref[...], cp_ref[...], sp_ref[...]

        def f(q, kc, kp, vc, vp, *sinks):
            return _att_block(q, kc, kp, vc, vp, sinks, cc, sc, cp, sp, n == 0)

        sinks = [sk_ref[0, g] for g in range(ATT_GROUP)]
        _, vjp = jax.vjp(f, q_ref[...], kc_ref[...], kp_ref[...], vc_ref[...], vp_ref[...], *sinks)
        ct = vjp(do_ref[...].astype(f32))
        dq_ref[...] = ct[0].astype(dq_ref.dtype)
        dkc_ref[...] = ct[1].astype(dkc_ref.dtype)
        dkp_ref[...] = ct[2].astype(dkp_ref.dtype)
        dvc_ref[...] = ct[3].astype(dvc_ref.dtype)
        dvp_ref[...] = ct[4].astype(dvp_ref.dtype)

        @pl.when(n == 0)
        def _():
            dsk_ref[...] = jnp.zeros_like(dsk_ref)

        for g in range(ATT_GROUP):
            dsk_ref[0, g] += ct[5 + g]

    kv_spec = pl.BlockSpec((ATT_BLOCK, ATT_HEAD), lambda h, n: (n, h))
    return pl.pallas_call(
        body, name="att_bwd", grid=(hkv, s // ATT_BLOCK),
        in_specs=_att_specs(qd, kvd) + [pl.BlockSpec((ATT_BLOCK, gw), lambda h, n: (n, h))],
        out_specs=[pl.BlockSpec((ATT_BLOCK, gw), lambda h, n: (n, h)), kv_spec, kv_spec, kv_spec, kv_spec,
                   pl.BlockSpec((1, ATT_GROUP, 1, 1), lambda h, n: (h, 0, 0, 0))],
        out_shape=[jax.ShapeDtypeStruct((s, qd), f32)] + [jax.ShapeDtypeStruct((s, kvd), f32)] * 4
        + [jax.ShapeDtypeStruct((hkv, ATT_GROUP, 1, 1), f32)],
        compiler_params=_cp(("parallel", "arbitrary")),
    )(qkv, qkv, qkv, qkv, qkv, sinks4, cos, sin, cos, sin, do)


def _att_dqkv(dq, dkc, dkp, dvc, dvp):
    s, qd = dq.shape
    kvd = dkc.shape[1]
    nb = s // ATT_BLOCK

    def body(dq_ref, dkc_ref, dkp_ref, dvc_ref, dvp_ref, o_ref, b_ref):
        n = pl.program_id(0)
        keep = n < nb - 1
        dk = dkc_ref[...] + jnp.where(keep, dkp_ref[...], 0.0)
        dv = dvc_ref[...] + jnp.where(keep, dvp_ref[...], 0.0)
        d = jnp.concatenate([dq_ref[...], dk, dv], axis=-1)
        o_ref[...] = d.astype(o_ref.dtype)

        @pl.when(n == 0)
        def _():
            b_ref[...] = jnp.zeros_like(b_ref)

        b_ref[...] += _colsum(d)

    cur = lambda w: pl.BlockSpec((ATT_BLOCK, w), lambda n: (n, 0))
    nxt = lambda w: pl.BlockSpec((ATT_BLOCK, w), lambda n: (jnp.minimum(n + 1, nb - 1), 0))
    w = qd + 2 * kvd
    return pl.pallas_call(
        body, name="att_dqkv", grid=(nb,), in_specs=[cur(qd), cur(kvd), nxt(kvd), cur(kvd), nxt(kvd)],
        out_specs=[cur(w), pl.BlockSpec((1, w), lambda n: (0, 0))],
        out_shape=[jax.ShapeDtypeStruct((s, w), bf16), jax.ShapeDtypeStruct((1, w), f32)],
        compiler_params=_cp(("arbitrary",)),
    )(dq, dkc, dkp, dvc, dvp)


def _sdot(a, b, dims):
    return lax.dot_general(a, b, (dims, ((), ())), precision=SCAN_PREC, preferred_element_type=f32)


def _cdot(a, b, dims):
    return lax.dot_general(a, b, (dims, ((), ())), precision=CHAIN_PREC, preferred_element_type=f32)


def _adot(a, b, dims):
    return lax.dot_general(a, b, (dims, ((), ())), precision=APPLY_PREC, preferred_element_type=f32)


def _bdot(a, b, dims):
    return lax.dot_general(a, b, (dims, ((), ())), precision=BULK_PREC, preferred_element_type=f32)


_NN = ((1,), (0,))
_NT = ((1,), (1,))
_TN = ((0,), (0,))


def _head_mask(wd):
    li = lax.broadcasted_iota(jnp.int32, (wd, wd), 0) // RWKV_HEAD
    lj = lax.broadcasted_iota(jnp.int32, (wd, wd), 1) // RWKV_HEAD
    return (li == lj).astype(f32)


def _stack_impl(t):
    wd = t.shape[1]
    return jnp.concatenate([t] * (wd // RWKV_HEAD), axis=0) * _head_mask(wd)


def _unstack_impl(m):
    wd = m.shape[1]
    m = m * _head_mask(wd)
    return sum(m[h * RWKV_HEAD:(h + 1) * RWKV_HEAD] for h in range(wd // RWKV_HEAD))


@jax.custom_vjp
def _stack(t):
    return _stack_impl(t)


@jax.custom_vjp
def _unstack(m):
    return _unstack_impl(m)


_stack.defvjp(lambda t: (_stack_impl(t), None), lambda _, dm: (_unstack_impl(dm),))
_unstack.defvjp(lambda m: (_unstack_impl(m), None), lambda _, dt: (_stack_impl(dt),))


def _unit_lower_inverse(t):
    L, wd = t.shape
    qt = lax.broadcasted_iota(jnp.int32, (L, wd), 0)
    qi = lax.broadcasted_iota(jnp.int32, (L, wd), 1) % L
    x = (qt == qi).astype(f32) + t
    p = _cdot(t, _stack(t), _NN)
    for _ in range(int(math.log2(L)) - 2):
        both = _cdot(p, jnp.concatenate([_stack(x), _stack(p)], axis=1), _NN)
        x, p = x + both[:, :wd], both[:, wd:]
    return x + _cdot(p, _stack(x), _NN)


@jax.custom_vjp
def _solve_unit_lower(t, rhs):
    return _cdot(_unit_lower_inverse(t), _stack(rhs), _NN)


def _solve_fwd(t, rhs):
    minv = _unit_lower_inverse(t)
    u = _cdot(minv, _stack(rhs), _NN)
    return u, (minv, u)


def _solve_bwd(res, du):
    minv, u = res
    drhs = _unstack(_cdot(minv, du, _TN))
    return _cdot(drhs, _stack(u), _NT), drhs


_solve_unit_lower.defvjp(_solve_fwd, _solve_bwd)


def _rwkv_chunk(s0, xr, xk, xv, wp, ap, g, w0, a0, k_k, k_a, ln_w, ln_b, r_k):
    L, wd = xr.shape
    n = RWKV_HEAD
    assert L == n
    hsum = lambda t: _sdot(t, _head_mask(wd), _NN)
    stack = _stack

    wl = -jax.nn.softplus(-(w0 + wp)) - 0.5
    a = _sigmoid(a0 + ap)
    kkv = xk * k_k
    k = xk * (1.0 + (a - 1.0) * k_a)
    early = hsum(jnp.concatenate([kkv * kkv, xr * k * r_k], axis=0))
    kk = kkv / jnp.maximum(jnp.sqrt(early[:L]), 1e-12)
    lw = -jnp.exp(wl)
    ti = lax.broadcasted_iota(jnp.int32, (L, L), 0)
    tj = lax.broadcasted_iota(jnp.int32, (L, L), 1)
    cum = _sdot((tj <= ti).astype(f32), lw, _NN)
    w_t, w_prev, w_inv = jnp.exp(cum), jnp.exp(cum - lw), jnp.exp(-cum)
    at, rt, bt, kt = -kk * w_prev, xr * w_t, kk * a * w_inv, k * w_inv
    b4, k4, v4, s4 = stack(bt), stack(kt), stack(xv), stack(s0)

    qt = lax.broadcasted_iota(jnp.int32, (L, wd), 0)
    qi = lax.broadcasted_iota(jnp.int32, (L, wd), 1) % L
    strict, incl = qi < qt, qi <= qt
    ar = jnp.concatenate([at, rt], axis=0)
    prod = _bdot(ar, jnp.concatenate([b4, k4], axis=0), _NT)
    t_ab, t_ak = jnp.where(strict, prod[:L, :wd], 0.0), jnp.where(strict, prod[:L, wd:], 0.0)
    r_b, r_k4 = jnp.where(incl, prod[L:, :wd], 0.0), jnp.where(incl, prod[L:, wd:], 0.0)
    on_s = _adot(ar, s4, _NT)
    on_v = _adot(jnp.concatenate([t_ak, r_k4], axis=0), v4, _NN)
    u = _solve_unit_lower(t_ab, on_s[:L] + on_v[:L])
    y = on_s[L:] + _adot(r_b, stack(u), _NN) + on_v[L:]
    upd = _unstack(_adot(jnp.concatenate([u, xv], axis=0), jnp.concatenate([bt, kt], axis=0), _TN))
    s_new = (s0 + upd) * w_t[L - 1:L, :]

    mu = hsum(y) * (1.0 / n)
    d = y - mu
    var = hsum(d * d) * (1.0 / n)
    yn = d * lax.rsqrt(var + GN_EPS) * ln_w + ln_b
    bonus = early[L:] * xv
    return s_new, (yn + bonus) * g


def _rwkv_chunks(*args):
    gw = RWKV_GROUP * RWKV_HEAD
    res = [_rwkv_chunk(*[t[:, i * gw:(i + 1) * gw] for t in args]) for i in range(args[0].shape[1] // gw)]
    return jnp.concatenate([r[0] for r in res], axis=1), jnp.concatenate([r[1] for r in res], axis=1)


def _rwkv_specs(d, L, rev, nchunk):
    gw = min(RWKV_LANES, d)
    nb = d // gw
    cc = (lambda c: nchunk - 1 - c) if rev else (lambda c: c)
    tok = lambda off: pl.BlockSpec((L, gw), lambda p, c: (cc(c), p + off))
    par = pl.BlockSpec((1, gw), lambda p, c: (0, p))
    return [tok(0), tok(nb), tok(2 * nb), tok(0), tok(0), tok(0)] + [par] * 7, tok(0), par, cc


def _rwkv_fwd(xs_rkv, wpre, apre, gate, params, d):
    s = xs_rkv.shape[0]
    L = RWKV_CHUNK
    nchunk = s // L
    gw = min(RWKV_LANES, d)
    in_specs, tok, par, _ = _rwkv_specs(d, L, False, nchunk)

    def body(*refs):
        ins, (o_ref, ck_ref, st_ref) = refs[:13], refs[13:]

        @pl.when(pl.program_id(1) == 0)
        def _():
            st_ref[...] = jnp.zeros_like(st_ref)

        s0 = st_ref[...]
        ck_ref[0] = s0
        s_new, out = _rwkv_chunks(s0, *[r[...] for r in ins])
        st_ref[...] = s_new
        o_ref[...] = out.astype(o_ref.dtype)

    return pl.pallas_call(
        body, name="rwkv_fwd", grid=(d // gw, nchunk), in_specs=in_specs,
        out_specs=[tok, pl.BlockSpec((1, RWKV_HEAD, gw), lambda p, c: (c, 0, p))],
        out_shape=[jax.ShapeDtypeStruct((s, d), bf16), jax.ShapeDtypeStruct((nchunk, RWKV_HEAD, d), f32)],
        scratch_shapes=[pltpu.VMEM((RWKV_HEAD, gw), f32)], compiler_params=_cp(("parallel", "arbitrary")),
    )(xs_rkv, xs_rkv, xs_rkv, wpre, apre, gate, *params)


def _rwkv_bwd(xs_rkv, wpre, apre, gate, params, ckpt, dout, d):
    s = xs_rkv.shape[0]
    L = RWKV_CHUNK
    nchunk = s // L
    gw = min(RWKV_LANES, d)
    in_specs, tok, par, cc = _rwkv_specs(d, L, True, nchunk)
    in_specs = in_specs + [pl.BlockSpec((1, RWKV_HEAD, gw), lambda p, c: (cc(c), 0, p)), tok]

    def body(*refs):
        ins, ck_ref, do_ref = refs[:13], refs[13], refs[14]
        outs, ds_ref = refs[15:28], refs[28]
        first = pl.program_id(1) == 0

        @pl.when(first)
        def _():
            ds_ref[...] = jnp.zeros_like(ds_ref)

        _, vjp = jax.vjp(_rwkv_chunks, ck_ref[0], *[r[...] for r in ins])
        ct = vjp((ds_ref[...], do_ref[...].astype(f32)))
        ds_ref[...] = ct[0]
        for r, v in zip(outs[:6], ct[1:7]):
            r[...] = v.astype(r.dtype)

        @pl.when(first)
        def _():
            for r in outs[6:]:
                r[...] = jnp.zeros_like(r)

        for r, v in zip(outs[6:], ct[7:]):
            r[...] += v

    return pl.pallas_call(
        body, name="rwkv_bwd", grid=(d // gw, nchunk), in_specs=in_specs, out_specs=[tok] * 6 + [par] * 7,
        out_shape=[jax.ShapeDtypeStruct((s, d), f32)] * 3 + [jax.ShapeDtypeStruct((s, d), bf16)] * 3
        + [jax.ShapeDtypeStruct((1, d), f32)] * 7,
        scratch_shapes=[pltpu.VMEM((RWKV_HEAD, gw), f32)], compiler_params=_cp(("parallel", "arbitrary")),
    )(xs_rkv, xs_rkv, xs_rkv, wpre, apre, gate, *params, ckpt, dout)


def _all_gather(xs, *, name):
    na = len(xs)

    def body(*refs):
        x_refs, o_refs = refs[:na], refs[na:2 * na]
        send_sems, recv_sems, local_sems = refs[2 * na:]
        x, y, c = lax.axis_index("x"), lax.axis_index("y"), lax.axis_index("c")
        me, sibling = (x, y, c), (x, y, 1 - c)
        chips = [(1 - x, y), (x, 1 - y), (1 - x, 1 - y)]

        def blk(a, dev):
            return o_refs[a].at[4 * dev[0] + 2 * dev[1] + dev[2]]

        def copy(a, k, block, to, src=None):
            return pltpu.make_async_remote_copy(
                src_ref=blk(a, block) if src is None else src, dst_ref=blk(a, block),
                send_sem=send_sems.at[a, k], recv_sem=recv_sems.at[a, k], device_id=to, device_id_type=MESH_ID)

        mine = [pltpu.make_async_copy(x_refs[a], blk(a, me), local_sems.at[a]) for a in range(na)]
        for cp in mine:
            cp.start()
        first = []
        for a in range(na):
            first.append(copy(a, 0, me, sibling, src=x_refs[a]))
            first += [copy(a, 1 + j, me, (*chip, c), src=x_refs[a]) for j, chip in enumerate(chips)]
        for cp in first:
            cp.start()
        passed = []
        for a in range(na):
            for j, chip in enumerate(chips):
                copy(a, 1 + j, (*chip, c), me).wait_recv()
                fwd = copy(a, 4 + j, (*chip, c), sibling)
                fwd.start()
                passed.append(fwd)
        for a in range(na):
            copy(a, 0, sibling, me).wait_recv()
            for j, chip in enumerate(chips):
                copy(a, 4 + j, (*chip, 1 - c), me).wait_recv()
        for cp in first + passed:
            cp.wait_send()
        for cp in mine:
            cp.wait()

    anyspec = pl.BlockSpec(memory_space=pl.ANY)
    return pl.pallas_call(
        body, name=name, in_specs=[anyspec] * na, out_specs=[anyspec] * na,
        out_shape=[jax.ShapeDtypeStruct((N_DEV,) + t.shape, t.dtype) for t in xs],
        scratch_shapes=[pltpu.SemaphoreType.DMA((na, 7)), pltpu.SemaphoreType.DMA((na, 7)), pltpu.SemaphoreType.DMA((na,))],
    )(*xs)


def _exchange(xs, *, name):
    na = len(xs)

    def body(*refs):
        x_refs, o_refs = refs[:na], refs[na:2 * na]
        send_sems, recv_sems, local_sems = refs[2 * na:]
        x, y, c = lax.axis_index("x"), lax.axis_index("y"), lax.axis_index("c")
        my_id = 4 * x + 2 * y + c
        mine = [pltpu.make_async_copy(x_refs[a].at[my_id], o_refs[a].at[my_id], local_sems.at[a]) for a in range(na)]
        for cp in mine:
            cp.start()
        copies = []
        for a in range(na):
            for r in range(1, N_DEV):
                px, py, pc = x ^ (r >> 2), y ^ ((r >> 1) & 1), c ^ (r & 1)
                copies.append(pltpu.make_async_remote_copy(
                    src_ref=x_refs[a].at[4 * px + 2 * py + pc], dst_ref=o_refs[a].at[my_id],
                    send_sem=send_sems.at[a, r - 1], recv_sem=recv_sems.at[a, r - 1],
                    device_id=(px, py, pc), device_id_type=MESH_ID))
        for cp in copies:
            cp.start()
        for cp in copies:
            cp.wait_recv()
        for cp in copies:
            cp.wait_send()
        for cp in mine:
            cp.wait()

    anyspec = pl.BlockSpec(memory_space=pl.ANY)
    return pl.pallas_call(
        body, name=name, in_specs=[anyspec] * na, out_specs=[anyspec] * na,
        out_shape=[jax.ShapeDtypeStruct(t.shape, t.dtype) for t in xs],
        scratch_shapes=[pltpu.SemaphoreType.DMA((na, 7)), pltpu.SemaphoreType.DMA((na, 7)), pltpu.SemaphoreType.DMA((na,))],
    )(*xs)


def _peer(r):
    x, y, c = lax.axis_index("x"), lax.axis_index("y"), lax.axis_index("c")
    return x ^ (r >> 2), y ^ ((r >> 1) & 1), c ^ (r & 1)


def _my_id():
    return 4 * lax.axis_index("x") + 2 * lax.axis_index("y") + lax.axis_index("c")


def _split_copies(src_refs, land_refs, send_sems, recv_sems, gather):
    copies = []
    for a, (src, land) in enumerate(zip(src_refs, land_refs)):
        for r in range(1, N_DEV):
            px, py, pc = _peer(r)
            k = a * (N_DEV - 1) + r - 1
            copies.append(pltpu.make_async_remote_copy(
                src_ref=src if gather else src.at[4 * px + 2 * py + pc], dst_ref=land.at[_my_id()],
                send_sem=send_sems.at[k], recv_sem=recv_sems.at[k],
                device_id=(px, py, pc), device_id_type=MESH_ID))
    return copies


_HBM_SPEC = pl.BlockSpec(memory_space=pltpu.HBM)
_SEM_SPEC = pl.BlockSpec(memory_space=pltpu.SEMAPHORE)
_DATAFLOW = pltpu.SideEffectType.DATAFLOW_SIDE_EFFECTING


def _send_start(xs, after, *, gather, name):
    na = len(xs)
    lands = [lax.empty((N_DEV,) + t.shape if gather else t.shape, t.dtype) for t in xs]

    def body(*refs):
        src_refs, land_refs = refs[:na], refs[na:2 * na]
        send_sems, recv_sems = refs[2 * na + 1], refs[2 * na + 2]
        token = refs[-1]
        for cp in _split_copies(src_refs, land_refs, send_sems, recv_sems, gather):
            cp.start()
        token[...] = jnp.zeros_like(token)

    hbm = lambda t: pltpu.with_memory_space_constraint(t, pltpu.HBM)
    res = pl.pallas_call(
        body, name=name,
        out_shape=(pltpu.SemaphoreType.DMA((na * (N_DEV - 1),)), pltpu.SemaphoreType.DMA((na * (N_DEV - 1),)),
                   *[pltpu.HBM(t.shape, t.dtype) for t in xs], *[pltpu.HBM(t.shape, t.dtype) for t in lands],
                   jax.ShapeDtypeStruct((8, LANES), f32)),
        in_specs=[_HBM_SPEC] * (2 * na) + [pl.BlockSpec(memory_space=pl.ANY)],
        out_specs=(_SEM_SPEC, _SEM_SPEC, *[_HBM_SPEC] * (2 * na), pl.BlockSpec(memory_space=pltpu.VMEM)),
        input_output_aliases={i: 2 + i for i in range(2 * na)},
        compiler_params=pltpu.CompilerParams(has_side_effects=_DATAFLOW),
    )(*[hbm(t) for t in xs], *[hbm(t) for t in lands], after)
    return res[0], res[1], list(res[2:2 + na]), list(res[2 + na:2 + 2 * na]), res[-1]


def _send_wait(handle, after, *, gather, name):
    send_sems, recv_sems, srcs, lands, _ = handle
    na = len(srcs)

    def body(*refs):
        src_refs, land_refs = refs[:na], refs[na:2 * na]
        s_sems, r_sems = refs[2 * na], refs[2 * na + 1]
        for cp in _split_copies(src_refs, land_refs, s_sems, r_sems, gather):
            cp.wait_send()
            cp.wait_recv()

    res = pl.pallas_call(
        body, name=name,
        out_shape=tuple(pltpu.HBM(t.shape, t.dtype) for t in srcs + lands),
        in_specs=[_HBM_SPEC] * (2 * na) + [_SEM_SPEC, _SEM_SPEC, pl.BlockSpec(memory_space=pl.ANY)],
        out_specs=tuple([_HBM_SPEC] * (2 * na)), input_output_aliases={i: i for i in range(2 * na)},
        compiler_params=pltpu.CompilerParams(has_side_effects=_DATAFLOW),
    )(*srcs, *lands, send_sems, recv_sems, after)
    return list(res[:na]), list(res[na:])


def _fill_own(land, own):
    return lax.dynamic_update_index_in_dim(land, own, _my_id(), 0)


def _adamw(parts, w, m, v, *, name):
    r, c = w.shape
    tm = _pick(r, (64, 32, 16, 8)) if c > 2048 else _pick(r, (256, 128, 64, 32, 16, 8))
    b1c = 1.0 - ADAM_B1 ** ADAM_STEP
    b2c = 1.0 - ADAM_B2 ** ADAM_STEP

    def body(p_ref, w_ref, m_ref, v_ref, g_ref, d_ref, nm_ref, nv_ref):
        g = p_ref[0].astype(f32)
        for s in range(1, N_DEV):
            g = g + p_ref[s].astype(f32)
        mm = ADAM_B1 * m_ref[...] + (1.0 - ADAM_B1) * g
        vv = ADAM_B2 * v_ref[...] + (1.0 - ADAM_B2) * (g * g)
        g_ref[...] = g
        nm_ref[...] = mm
        nv_ref[...] = vv
        d_ref[...] = -ADAM_LR * ((mm / b1c) / (jnp.sqrt(vv / b2c) + ADAM_EPS) + ADAM_WD * w_ref[...])

    spec = pl.BlockSpec((tm, c), lambda i: (i, 0))
    return pl.pallas_call(
        body, name=name, grid=(r // tm,),
        in_specs=[pl.BlockSpec((N_DEV, tm, c), lambda i: (0, i, 0)), spec, spec, spec], out_specs=[spec] * 4,
        out_shape=[jax.ShapeDtypeStruct((r, c), f32)] * 4, compiler_params=_cp(("parallel",)),
    )(parts, w, m, v)


WEIGHTS = ['norm_mix_pre', 'norm_mix_post', 'norm_ffn_pre', 'norm_ffn_post', 'w_in', 'b_qkv', 'att_sinks', 'mu_shift', 'w0',
           'w2', 'a0', 'a2', 'g2', 'k_k', 'k_a', 'r_k', 'ln_x_w', 'ln_x_b', 'w_att_branch', 'w_rwkv_branch', 'w_out',
           'w_ffn_gate', 'w_ffn_up', 'w_ffn_down']
SHARDED = ['w_in', 'w2', 'a2', 'g2', 'w_att_branch', 'w_rwkv_branch', 'w_out', 'w_ffn_gate', 'w_ffn_up', 'w_ffn_down']
FIRST_WEIGHTS = ['w_in', 'w2', 'a2', 'g2']
BRANCH_WEIGHTS = ['w_att_branch', 'w_rwkv_branch', 'w_out']
FFN_WEIGHTS = ['w_ffn_gate', 'w_ffn_up', 'w_ffn_down']
COL_SHARDED = ['w_in', 'w2', 'a2', 'g2', 'w_ffn_gate', 'w_ffn_up']


def _cols_to_blocks(t):
    r, c = t.shape
    return jnp.moveaxis(t.reshape(r, N_DEV, c // N_DEV), 1, 0)


def _blocks_to_cols(t):
    n, r, c = t.shape
    return jnp.moveaxis(t, 0, 1).reshape(r, n * c)


def _cols_of_blocks(blocks, c0, c1):
    bw = blocks.shape[2]
    parts = []
    for j in range(blocks.shape[0]):
        lo, hi = max(c0, j * bw), min(c1, (j + 1) * bw)
        if lo < hi:
            parts.append(blocks[j, :, lo - j * bw:hi - j * bw])
    return parts[0] if len(parts) == 1 else jnp.concatenate(parts, axis=1)


def _blocks_of_cols(pieces, nblk):
    r = pieces[0].shape[0]
    bw = sum(p.shape[1] for p in pieces) // nblk
    out = lax.empty((nblk, r, bw), pieces[0].dtype)
    c0 = 0
    for p in pieces:
        for j in range(c0 // bw, (c0 + p.shape[1] - 1) // bw + 1):
            lo, hi = max(c0, j * bw), min(c0 + p.shape[1], (j + 1) * bw)
            out = lax.dynamic_update_slice(out, p[None, :, lo - c0:hi - c0], (j, 0, lo - j * bw))
        c0 += p.shape[1]
    return out


def _pad_cols(t, width):
    return t if t.shape[1] == width else jnp.pad(t, ((0, 0), (0, width - t.shape[1])))


def _pad_rows(t, height):
    return t if t.shape[0] == height else jnp.pad(t, ((0, height - t.shape[0]), (0, 0)))


def kernel(x, norm_mix_pre, norm_mix_post, norm_ffn_pre, norm_ffn_post, w_in, b_qkv, att_sinks, mu_shift, w0, w2, a0, a2, g2, k_k, k_a, r_k, ln_x_w, ln_x_b, w_att_branch, w_rwkv_branch, w_out, w_ffn_gate, w_ffn_up, w_ffn_down, loss_target, m_norm_mix_pre, m_norm_mix_post, m_norm_ffn_pre, m_norm_ffn_post, m_w_in, m_b_qkv, m_att_sinks, m_mu_shift, m_w0, m_w2, m_a0, m_a2, m_g2, m_k_k, m_k_a, m_r_k, m_ln_x_w, m_ln_x_b, m_w_att_branch, m_w_rwkv_branch, m_w_out, m_w_ffn_gate, m_w_ffn_up, m_w_ffn_down, v_norm_mix_pre, v_norm_mix_post, v_norm_ffn_pre, v_norm_ffn_post, v_w_in, v_b_qkv, v_att_sinks, v_mu_shift, v_w0, v_w2, v_a0, v_a2, v_g2, v_k_k, v_k_a, v_r_k, v_ln_x_w, v_ln_x_b, v_w_att_branch, v_w_rwkv_branch, v_w_out, v_w_ffn_gate, v_w_ffn_up, v_w_ffn_down):
    loc = dict(locals())
    wts = {n: loc[n] for n in WEIGHTS}
    mom_m = {n: loc["m_" + n] for n in WEIGHTS}
    mom_v = {n: loc["v_" + n] for n in WEIGHTS}

    xt = x[0]
    tgt = loss_target[0]
    s, d = xt.shape
    qkvd = b_qkv.shape[1]
    qd = w_att_branch.shape[1] * N_DEV
    kvd = (qkvd - qd) // 2
    dd, da, dg = w2.shape[1], a2.shape[1], g2.shape[1]
    ddp, dap, dgp = _round_up(dd, LANES), _round_up(da, LANES), _round_up(dg, LANES)
    lrp = dgp + ddp + dap
    hkv = kvd // ATT_HEAD
    tm_row = 128

    shards = {n: wts[n][0].astype(bf16) for n in SHARDED}
    gw = dict(zip(FIRST_WEIGHTS, _all_gather([shards[n] for n in FIRST_WEIGHTS], name="gather_w_in")))
    h_branch = _send_start([shards[n] for n in BRANCH_WEIGHTS], gw['w2'], gather=True, name="gather_branch_start")
    h_ffn = _send_start([shards[n] for n in FFN_WEIGHTS], h_branch[-1], gather=True, name="gather_ffn_start")
    started = h_ffn[-1][0:1, 0:1]
    wcols = functools.partial(_cols_of_blocks, gw['w_in'])
    lr0 = qkvd + 3 * d
    w_qkv = wcols(0, qkvd)
    w_rkv = wcols(qkvd, lr0)
    w_lr = jnp.concatenate([_pad_cols(wcols(lr0 + dd + da, lr0 + dd + da + dg), dgp), _pad_cols(wcols(lr0, lr0 + dd), ddp),
                            _pad_cols(wcols(lr0 + dd, lr0 + dd + da), dap)], axis=1)
    w_gates = wcols(lr0 + dd + da + dg, lr0 + dd + da + dg + 2 * d)
    w2_f = _pad_rows(_blocks_to_cols(gw['w2']), ddp)
    a2_f = _pad_rows(_blocks_to_cols(gw['a2']), dap)
    g2_f = _pad_rows(_blocks_to_cols(gw['g2']), dgp)

    mu = mu_shift
    mu_rkv = mu[:, :3 * d]
    mu_lr = jnp.concatenate([_pad_cols(mu[:, 3 * d + dd + da:], dgp), _pad_cols(mu[:, 3 * d:3 * d + dd], ddp),
                             _pad_cols(mu[:, 3 * d + dd:3 * d + dd + da], dap)], axis=1)
    pos = jnp.arange(s, dtype=f32)
    inv_freq = ROPE_THETA ** (-jnp.arange(0, ATT_HEAD, 2, dtype=f32) / ATT_HEAD)
    ang = pos[:, None] * inv_freq[None, :]
    cos, sin = jnp.cos(ang), jnp.sin(ang)
    sinks4 = att_sinks.reshape(hkv, ATT_GROUP, 1, 1)
    rw_params = [w0, a0, k_k, k_a, ln_x_w, ln_x_b, r_k.reshape(1, d)]

    (h1,), _ = _rowwise(lambda xv, g: ((_rms(xv, g),), ()), [xt], [norm_mix_pre + started], [(d, bf16)], [], tm=tm_row,
                        name="norm_mix_pre")
    qkv = _mm(h1, w_qkv, bias=b_qkv, name="proj_qkv")
    p_rkv = _mm(h1, w_rkv, name="proj_rkv")
    p_lr = _mm(h1, w_lr, name="proj_lr")
    p_gates = _mm(h1, w_gates, name="proj_gates")

    o_att = _att_fwd(qkv, sinks4, cos, sin, qd, kvd)

    xs_rkv = _shift_fwd(p_rkv, mu_rkv, name="shift_rkv")
    xs_lr = _shift_fwd(p_lr, mu_lr, name="shift_lr")

    def lr_act(v):
        return jnp.concatenate([_sigmoid(v[:, :dgp]), jnp.tanh(v[:, dgp:dgp + ddp]), v[:, dgp + ddp:]], axis=-1)

    (act_lr,), _ = _rowwise(lambda v: ((lr_act(v),), ()), [xs_lr], [], [(lrp, bf16)], [], tm=256, name="lowrank_act")
    gate_r = _mm(act_lr, g2_f, a_col0=0, a_cols=dgp, name="lowrank_g")
    wpre = _mm(act_lr, w2_f, a_col0=dgp, a_cols=ddp, name="lowrank_w")
    apre = _mm(act_lr, a2_f, a_col0=dgp + ddp, a_cols=dap, name="lowrank_a")
    o_rwkv, ckpt = _rwkv_fwd(xs_rkv, wpre, apre, gate_r, rw_params, d)

    srcs, lands = _send_wait(h_branch, o_rwkv, gather=True, name="gather_branch_wait")
    w_ab, w_rb, w_o = [_fill_own(l, t).reshape(-1, d) for l, t in zip(lands, srcs)]
    oa = _mm(o_att, w_ab, name="att_branch")
    ob = _mm(o_rwkv, w_rb, name="rwkv_branch")

    def merge(oav, obv, gv):
        return _sigmoid(gv[:, :d]) * oav + _sigmoid(gv[:, d:]) * obv

    (merged,), _ = _rowwise(lambda oav, obv, gv: ((merge(oav, obv, gv),), ()), [oa, ob, p_gates], [], [(d, bf16)], [],
                            tm=tm_row, name="merge")
    mo = _mm(merged, w_o, name="out_proj")

    def mid(xv, mov, g_post, g_pre):
        x2 = xv + _rms(mov, g_post)
        return (x2, _rms(x2, g_pre)), ()

    (x2, h2), _ = _rowwise(mid, [xt, mo], [norm_mix_post, norm_ffn_pre], [(d, f32), (d, bf16)], [], tm=tm_row, name="mid_norms")
    srcs, lands = _send_wait(h_ffn, h2, gather=True, name="gather_ffn_wait")
    w_fg, w_fu, w_fd = [_fill_own(l, t) for l, t in zip(lands, srcs)]
    fgate = _mm(h2, w_fg, o_blk=True, name="ffn_gate")
    fup = _mm(h2, w_fu, o_blk=True, name="ffn_up")
    fb = fgate.shape[2]

    def swiglu(gv, uv):
        return gv * _sigmoid(gv) * uv

    (act,), _ = _rowwise(lambda gv, uv: ((swiglu(gv, uv),), ()), [fgate.reshape(N_DEV * s, fb), fup.reshape(N_DEV * s, fb)], [],
                         [(fb, bf16)], [], tm=512, name="swiglu")
    act3 = act.reshape(N_DEV, s, fb)
    fo = _mm(act3, w_fd, name="ffn_down")

    def head(fv, x2v, tv, g_post):
        yv, vjp = jax.vjp(lambda f_, g_: _rms(f_, g_), fv, g_post)
        err = x2v + yv - tv
        dy = err * (1.0 / d)
        df, dgp_ = vjp(dy)
        lossp = 0.5 * jnp.sum(jnp.sum(err * err, axis=-1, keepdims=True) * (1.0 / d), axis=0, keepdims=True)
        return (df, dy), (dgp_, lossp)

    (dfo, dy), (g_norm_ffn_post, loss_part) = _rowwise(head, [fo, x2, tgt], [norm_ffn_post], [(d, bf16), (d, f32)],
                                                       [(1, d), (1, 1)], tm=tm_row, name="loss_head")
    dact = _mm(dfo, w_fd, tb=True, o_blk=True, out_dtype=bf16, name="d_act")
    g_fd = _mm(act3, dfo, ta=True, o_blk=True, out_dtype=bf16, name="g_ffn_down")

    def swiglu_bwd(gv, uv, dav):
        _, vjp = jax.vjp(swiglu, gv, uv)
        return vjp(dav.astype(f32)), ()

    (dgate, dup), _ = _rowwise(swiglu_bwd, [fgate.reshape(N_DEV * s, fb), fup.reshape(N_DEV * s, fb), dact.reshape(N_DEV * s, fb)],
                               [], [(fb, bf16), (fb, bf16)], [], tm=512, name="swiglu_bwd")
    dgate3, dup3 = dgate.reshape(N_DEV, s, fb), dup.reshape(N_DEV, s, fb)
    g_fg = _mm(h2, dgate3, ta=True, o_blk=True, out_dtype=bf16, name="g_ffn_gate")
    g_fu = _mm(h2, dup3, ta=True, o_blk=True, out_dtype=bf16, name="g_ffn_up")
    h_g_ffn = _send_start([g_fg, g_fu, g_fd], loss_part, gather=False, name="exchange_ffn_start")
    started = h_g_ffn[-1][0:1, 0:1]
    dh2 = _mm(dgate3, w_fg, tb=True, name="d_h2_gate")
    dh2 = _mm(dup3, w_fu, tb=True, add=dh2, name="d_h2_up")

    def mid_bwd(xv, mov, dyv, dh2v, g_post, g_pre):
        x2v = xv + _rms(mov, g_post)
        _, vjp2 = jax.vjp(lambda t, g_: _rms(t, g_), x2v, g_pre)
        dx2a, dg_pre = vjp2(dh2v)
        dx2 = dyv + dx2a
        _, vjp1 = jax.vjp(lambda t, g_: _rms(t, g_), mov, g_post)
        dmo, dg_post = vjp1(dx2)
        return (dx2, dmo), (dg_pre, dg_post)

    (dx2, dmo), (g_norm_ffn_pre, g_norm_mix_post) = _rowwise(
        mid_bwd, [xt, mo, dy, dh2], [norm_mix_post, norm_ffn_pre + started], [(d, f32), (d, bf16)], [(1, d), (1, d)], tm=tm_row,
        name="mid_norms_bwd")
    dmerged = _mm(dmo, w_o, tb=True, name="d_merged")
    g_wo = _mm(merged, dmo, ta=True, out_dtype=bf16, name="g_w_out")

    def merge_bwd(oav, obv, gv, dmv):
        _, vjp = jax.vjp(merge, oav, obv, gv)
        return vjp(dmv), ()

    (doa, dob, dgates), _ = _rowwise(merge_bwd, [oa, ob, p_gates, dmerged], [], [(d, bf16), (d, bf16), (2 * d, bf16)], [],
                                     tm=tm_row, name="merge_bwd")
    do_att = _mm(doa, w_ab, tb=True, out_dtype=bf16, name="d_o_att")
    g_wab = _mm(o_att, doa, ta=True, out_dtype=bf16, name="g_w_att_branch")
    do_rwkv = _mm(dob, w_rb, tb=True, out_dtype=bf16, name="d_o_rwkv")
    g_wrb = _mm(o_rwkv, dob, ta=True, out_dtype=bf16, name="g_w_rwkv_branch")

    h_g_branch = _send_start([g_wab.reshape(N_DEV, -1, d), g_wrb.reshape(N_DEV, -1, d), g_wo.reshape(N_DEV, -1, d)], loss_part,
                             gather=False, name="exchange_branch_start")
    started = h_g_branch[-1][0:1, 0:1].reshape(1, 1, 1, 1)
    dq, dkc, dkp, dvc, dvp, dsinks = _att_bwd(qkv, sinks4 + started, cos, sin, do_att, qd, kvd)
    dqkv, g_b_qkv = _att_dqkv(dq, dkc, dkp, dvc, dvp)

    rb = _rwkv_bwd(xs_rkv, wpre, apre, gate_r, rw_params, ckpt, do_rwkv, d)
    dxs_r, dxs_k, dxs_v, dwpre, dapre, dgate_r = rb[:6]
    g_w0, g_a0, g_kk, g_ka, g_lnw, g_lnb, g_rk = rb[6:]
    g_g2 = _mm(act_lr, dgate_r, ta=True, name="g_g2")[:dgp]
    g_lr_rows = None
    d_sg = _mm(dgate_r, g2_f, tb=True, name="d_lowrank_g")
    d_tw = _mm(dwpre, w2_f, tb=True, name="d_lowrank_w")
    d_xa = _mm(dapre, a2_f, tb=True, name="d_lowrank_a")
    g_w2 = _mm(act_lr, dwpre, ta=True, name="g_w2")[dgp:dgp + ddp]
    g_a2 = _mm(act_lr, dapre, ta=True, name="g_a2")[dgp + ddp:]

    def lr_bwd(v, dsv, dtv, dav):
        _, vjp = jax.vjp(lr_act, v)
        return vjp(jnp.concatenate([dsv, dtv, dav], axis=-1)), ()

    (dxs_lr,), _ = _rowwise(lr_bwd, [xs_lr, d_sg, d_tw, d_xa], [], [(lrp, f32)], [], tm=256, name="lowrank_act_bwd")

    d_r, gmu_r = _shift_bwd(p_rkv, 0, dxs_r, mu_rkv[:, :d], name="shift_bwd_r")
    d_k, gmu_k = _shift_bwd(p_rkv, d, dxs_k, mu_rkv[:, d:2 * d], name="shift_bwd_k")
    d_v, gmu_v = _shift_bwd(p_rkv, 2 * d, dxs_v, mu_rkv[:, 2 * d:], name="shift_bwd_v")
    d_lr, gmu_lr = _shift_bwd(p_lr, 0, dxs_lr, mu_lr, name="shift_bwd_lr")

    gw_qkv = _mm(h1, dqkv, ta=True, out_dtype=bf16, name="g_w_qkv")
    gw_r = _mm(h1, d_r, ta=True, out_dtype=bf16, name="g_w_r")
    gw_k = _mm(h1, d_k, ta=True, out_dtype=bf16, name="g_w_k")
    gw_v = _mm(h1, d_v, ta=True, out_dtype=bf16, name="g_w_v")
    gw_lr = _mm(h1, d_lr, ta=True, out_dtype=bf16, name="g_w_lr")
    gw_gates = _mm(h1, dgates, ta=True, out_dtype=bf16, name="g_w_gates")
    g_w_in = _blocks_of_cols([gw_qkv, gw_r, gw_k, gw_v, gw_lr[:, dgp:dgp + dd], gw_lr[:, dgp + ddp:dgp + ddp + da],
                              gw_lr[:, :dg], gw_gates], N_DEV)
    send = {'w_in': g_w_in, 'w2': _cols_to_blocks(g_w2[:dd].astype(bf16)),
            'a2': _cols_to_blocks(g_a2[:da].astype(bf16)), 'g2': _cols_to_blocks(g_g2[:dg].astype(bf16))}
    h_g_in = _send_start([send[n] for n in FIRST_WEIGHTS], loss_part, gather=False, name="exchange_w_in_start")
    started = h_g_in[-1][0:1, 0:1]

    dh1 = _mm(dqkv, w_qkv, tb=True, bias=jnp.zeros((1, d), f32) + started, name="d_h1_qkv")
    dh1 = _mm(d_r, w_rkv[:, :d], tb=True, add=dh1, name="d_h1_r")
    dh1 = _mm(d_k, w_rkv[:, d:2 * d], tb=True, add=dh1, name="d_h1_k")
    dh1 = _mm(d_v, w_rkv[:, 2 * d:], tb=True, add=dh1, name="d_h1_v")
    dh1 = _mm(d_lr, w_lr, tb=True, add=dh1, name="d_h1_lr")
    dh1 = _mm(dgates, w_gates, tb=True, add=dh1, name="d_h1_gates")

    def pre_bwd(xv, dx2v, dh1v, g_pre):
        _, vjp = jax.vjp(lambda t, g_: _rms(t, g_), xv, g_pre)
        dxa, dg_pre = vjp(dh1v)
        return (dx2v + dxa,), (dg_pre,)

    (grad_x,), (g_norm_mix_pre,) = _rowwise(pre_bwd, [xt, dx2, dh1], [norm_mix_pre], [(d, f32)], [(1, d)], tm=tm_row,
                                            name="norm_mix_pre_bwd")

    g_mu = jnp.concatenate([gmu_r, gmu_k, gmu_v, gmu_lr[:, dgp:dgp + dd], gmu_lr[:, dgp + ddp:dgp + ddp + da], gmu_lr[:, :dg]],
                           axis=1)
    small = {'norm_mix_pre': g_norm_mix_pre, 'norm_mix_post': g_norm_mix_post, 'norm_ffn_pre': g_norm_ffn_pre,
             'norm_ffn_post': g_norm_ffn_post, 'b_qkv': g_b_qkv, 'att_sinks': dsinks.reshape(1, -1), 'mu_shift': g_mu,
             'w0': g_w0, 'a0': g_a0, 'k_k': g_kk, 'k_a': g_ka, 'r_k': g_rk, 'ln_x_w': g_lnw, 'ln_x_b': g_lnb}
    small_names = [n for n in WEIGHTS if n not in SHARDED]
    sizes = [wts[n].size for n in small_names]
    total = sum(sizes)
    rows_small = _round_up(_round_up(total, LANES) // LANES, 8)

    def pack(parts):
        flat = jnp.concatenate([p.reshape(-1).astype(f32) for p in parts])
        return jnp.pad(flat, (0, rows_small * LANES - total)).reshape(rows_small, LANES)

    h_small = _send_start([pack([small[n] for n in small_names])], loss_part, gather=True, name="gather_small_start")

    out_g, out_d, out_m, out_v = {}, {}, {}, {}
    after = h_small[-1]

    def update_group(names, handle, nm, after):
        srcs, lands = _send_wait(handle, after, gather=False, name=nm)
        for n, t, l in zip(names, srcs, lands):
            parts = _fill_own(l, lax.dynamic_index_in_dim(t, _my_id(), 0, keepdims=False))
            res = _adamw(parts, wts[n][0], mom_m[n][0], mom_v[n][0], name="adamw_" + n)
            for dst, src in zip((out_g, out_d, out_m, out_v), res):
                dst[n] = src.reshape(wts[n].shape)
        return res[0]

    after = update_group(FFN_WEIGHTS, h_g_ffn, "exchange_ffn_wait", after)
    after = update_group(BRANCH_WEIGHTS, h_g_branch, "exchange_branch_wait", after)
    srcs, lands = _send_wait(h_small, after, gather=True, name="gather_small_wait")
    sg, sd, sm, sv = _adamw(_fill_own(lands[0], srcs[0]), pack([wts[n] for n in small_names]),
                            pack([mom_m[n] for n in small_names]), pack([mom_v[n] for n in small_names]), name="adamw_small")
    off = 0
    for n, sz in zip(small_names, sizes):
        for dst, src in ((out_g, sg), (out_d, sd), (out_m, sm), (out_v, sv)):
            dst[n] = src.reshape(-1)[off:off + sz].reshape(wts[n].shape)
        off += sz
    update_group(FIRST_WEIGHTS, h_g_in, "exchange_w_in_wait", sg)

    loss = lax.psum(loss_part[0, 0], MESH_AXES)
    return (loss, grad_x[None], *[out_g[n] for n in WEIGHTS], *[out_d[n] for n in WEIGHTS],
            *[out_m[n] for n in WEIGHTS], *[out_v[n] for n in WEIGHTS])
```
